```python
import jax, jax.numpy as jnp
from jax import lax
import numpy as np

D_MODEL = 2048
BATCH = 8
SEQ = 4096
DEPTH = 1

HEAD_DIM = 128
MIX_WIDTH = D_MODEL
N_HEADS_A = 8
N_KV_A = 2
N_HEADS_B = 8
WIDTH_A = N_HEADS_A * HEAD_DIM
WIDTH_B = N_HEADS_B * HEAD_DIM
GRID_W = 64
AXIAL_THETA = 10000.0
ROPE_THETA = 500000.0
PARTIAL_ROT = HEAD_DIM // 4
DILATED_PAIRS = ((128, 1), (512, 4), (2048, 16))
Q_BLOCK = 128
D_FF = -(-8 * D_MODEL // (3 * 256)) * 256
EPS = 1e-6
PROJ_SIZES = (WIDTH_A, N_KV_A * HEAD_DIM, N_KV_A * HEAD_DIM, WIDTH_B, WIDTH_B, WIDTH_B)
PROJ_OUT = sum(PROJ_SIZES)

kernel_name = "hymba_axial_gqa_dilated_swiglu_encoder"


def rmsnorm(x, g):
    xf = x.astype(jnp.float32)
    y = xf * lax.rsqrt(jnp.mean(xf * xf, axis=-1, keepdims=True) + EPS)
    return (y * g.astype(jnp.float32)).astype(x.dtype)


def rope_angles(pos, dim, theta):
    inv = theta ** (-(jnp.arange(0, dim, 2, dtype=jnp.float32) / dim))
    return pos[:, None] * inv[None, :]


def apply_rope(x, ang):
    xf = x.astype(jnp.float32)
    cos = jnp.cos(ang)[None, :, None, :]
    sin = jnp.sin(ang)[None, :, None, :]
    x1, x2 = jnp.split(xf, 2, axis=-1)
    out = jnp.concatenate([x1 * cos - x2 * sin, x2 * cos + x1 * sin], axis=-1)
    return out.astype(x.dtype)


def axial_rope(x, seq_len):
    rows = seq_len // GRID_W
    row_pos = jnp.repeat(jnp.arange(rows, dtype=jnp.float32), GRID_W)
    col_pos = jnp.tile(jnp.arange(GRID_W, dtype=jnp.float32), rows)
    half = HEAD_DIM // 2
    xr = apply_rope(x[..., :half], rope_angles(row_pos, half, AXIAL_THETA))
    xc = apply_rope(x[..., half:], rope_angles(col_pos, half, AXIAL_THETA))
    return jnp.concatenate([xr, xc], axis=-1)


def partial_rope(x, seq_len):
    pos = jnp.arange(seq_len, dtype=jnp.float32)
    xr = apply_rope(x[..., :PARTIAL_ROT], rope_angles(pos, PARTIAL_ROT, ROPE_THETA))
    return jnp.concatenate([xr, x[..., PARTIAL_ROT:]], axis=-1)


def mixer_a(q, k, v, g_q, g_k):
    b, s, _, hd = q.shape
    grp = N_HEADS_A // N_KV_A
    q = axial_rope(rmsnorm(q, g_q), s) * (hd ** -0.5)
    k = axial_rope(rmsnorm(k, g_k), s)
    nb = s // Q_BLOCK
    qb = jnp.moveaxis(q.reshape(b, nb, Q_BLOCK, N_KV_A, grp, hd), 1, 0)

    def block(qblk):
        sc = jnp.einsum('bqkgd,bskd->bkgqs', qblk, k, preferred_element_type=jnp.float32)
        p = jax.nn.softmax(sc, axis=-1).astype(v.dtype)
        return jnp.einsum('bkgqs,bskd->bqkgd', p, v)

    o = lax.map(block, qb)
    return jnp.moveaxis(o, 0, 1).reshape(b, s, N_HEADS_A * hd)


def banded_attention(q, k, v, w):
    n, l, h, hd = q.shape
    nb = -(-l // w)
    lp = nb * w
    qp = jnp.pad(q, ((0, 0), (0, lp - l), (0, 0), (0, 0)))
    pad_k = ((0, 0), (w, lp - l + w), (0, 0), (0, 0))
    kb = jnp.pad(k, pad_k).reshape(n, nb + 2, w, h, hd)
    vb = jnp.pad(v, pad_k).reshape(n, nb + 2, w, h, hd)
    kwin = jnp.concatenate([kb[:, :-2], kb[:, 1:-1], kb[:, 2:]], axis=2)
    vwin = jnp.concatenate([vb[:, :-2], vb[:, 1:-1], vb[:, 2:]], axis=2)
    qb = qp.reshape(n, nb, w, h, hd)
    sc = jnp.einsum('nbqhd,nbkhd->nbhqk', qb, kwin, preferred_element_type=jnp.float32)
    start = jnp.arange(nb)[:, None, None] * w
    qpos = start + jnp.arange(w)[None, :, None]
    kpos = start - w + jnp.arange(3 * w)[None, None, :]
    valid = (jnp.abs(qpos - kpos) <= w) & (kpos >= 0) & (kpos < l)
    sc = jnp.where(valid[None, :, None], sc, -jnp.inf)
    lse = jax.nn.logsumexp(sc, axis=-1)
    p = jnp.exp(sc - lse[..., None]).astype(v.dtype)
    o = jnp.einsum('nbhqk,nbkhd->nbqhd', p, vwin).reshape(n, lp, h, hd)[:, :l]
    lse = jnp.transpose(lse, (0, 1, 3, 2)).reshape(n, lp, h)[:, :l]
    return o, lse


def dilated_branch(q, k, v, window, dilation):
    b, s, h, hd = q.shape
    l = s // dilation
    n_side = (window // 2) // dilation

    def to_res(t):
        return jnp.transpose(t.reshape(b, l, dilation, h, hd), (0, 2, 1, 3, 4)).reshape(b * dilation, l, h, hd)

    o, lse = banded_attention(to_res(q), to_res(k), to_res(v), n_side)
    o = jnp.transpose(o.reshape(b, dilation, l, h, hd), (0, 2, 1, 3, 4)).reshape(b, s, h, hd)
    lse = jnp.transpose(lse.reshape(b, dilation, l, h), (0, 2, 1, 3)).reshape(b, s, h)
    return o, lse


def mixer_b(q, k, v):
    b, s, h, hd = q.shape
    q = partial_rope(q, s) * (hd ** -0.5)
    k = partial_rope(k, s)
    outs, lses = [], []
    for window, dilation in DILATED_PAIRS:
        o, lse = dilated_branch(q, k, v, window, dilation)
        outs.append(o)
        lses.append(lse)
    wts = jax.nn.softmax(jnp.stack(lses, axis=0), axis=0)
    o = jnp.sum(wts[..., None].astype(v.dtype) * jnp.stack(outs, axis=0), axis=0)
    return o.reshape(b, s, h * hd)


def setup_inputs(seed: int = 0) -> dict:
    key = jax.random.key(seed)
    ks = jax.random.split(key, 12)
    f32 = jnp.float32

    def gain(k, n):
        return 1.0 + 0.02 * jax.random.normal(k, (DEPTH, n), f32)

    x = jax.random.normal(ks[0], (BATCH, SEQ, D_MODEL), f32)
    g_mix = gain(ks[1], D_MODEL)
    w_in = jax.random.normal(ks[2], (DEPTH, D_MODEL, PROJ_OUT), f32) * D_MODEL ** -0.5
    g_q_a = gain(ks[3], HEAD_DIM)
    g_k_a = gain(ks[4], HEAD_DIM)
    g_out_a = gain(ks[5], WIDTH_A)
    g_out_b = gain(ks[6], WIDTH_B)
    w_out = jax.random.normal(ks[7], (DEPTH, MIX_WIDTH, D_MODEL), f32) * (2.0 * MIX_WIDTH) ** -0.5
    g_ffn = gain(ks[8], D_MODEL)
    w_gate_up = jax.random.normal(ks[9], (DEPTH, D_MODEL, 2 * D_FF), f32) * D_MODEL ** -0.5
    w_down = jax.random.normal(ks[10], (DEPTH, D_FF, D_MODEL), f32) * (2.0 * D_FF) ** -0.5
    g_final = 1.0 + 0.02 * jax.random.normal(ks[11], (D_MODEL,), f32)
    return {"x": x, "g_mix": g_mix, "w_in": w_in, "g_q_a": g_q_a, "g_k_a": g_k_a,
            "g_out_a": g_out_a, "g_out_b": g_out_b, "w_out": w_out, "g_ffn": g_ffn,
            "w_gate_up": w_gate_up, "w_down": w_down, "g_final": g_final}


def reference(x, g_mix, w_in, g_q_a, g_k_a, g_out_a, g_out_b, w_out, g_ffn, w_gate_up, w_down, g_final):
    b, s, _ = x.shape
    offs = np.cumsum(PROJ_SIZES)[:-1].tolist()
    for layer in range(DEPTH):
        h = rmsnorm(x, g_mix[layer])
        proj = jnp.einsum('bsd,de->bse', h, w_in[layer])
        qa, ka, va, qb, kb, vb = jnp.split(proj, offs, axis=-1)
        qa = qa.reshape(b, s, N_HEADS_A, HEAD_DIM)
        ka = ka.reshape(b, s, N_KV_A, HEAD_DIM)
        va = va.reshape(b, s, N_KV_A, HEAD_DIM)
        qb = qb.reshape(b, s, N_HEADS_B, HEAD_DIM)
        kb = kb.reshape(b, s, N_HEADS_B, HEAD_DIM)
        vb = vb.reshape(b, s, N_HEADS_B, HEAD_DIM)
        out_a = rmsnorm(mixer_a(qa, ka, va, g_q_a[layer], g_k_a[layer]), g_out_a[layer])
        out_b = rmsnorm(mixer_b(qb, kb, vb), g_out_b[layer])
        mixed = jnp.concatenate([out_a, out_b], axis=-1)
        x = x + jnp.einsum('bse,ed->bsd', mixed, w_out[layer])
        h2 = rmsnorm(x, g_ffn[layer])
        gate, up = jnp.split(jnp.einsum('bsd,df->bsf', h2, w_gate_up[layer]), 2, axis=-1)
        x = x + jnp.einsum('bsf,fd->bsd', jax.nn.silu(gate) * up, w_down[layer])
    return rmsnorm(x, g_final)
```

```python
import functools
import math

import jax
import jax.numpy as jnp
from jax import lax
from jax.experimental import pallas as pl
from jax.experimental.pallas import tpu as pltpu

F32 = jnp.float32
BF16 = jnp.bfloat16

D_MODEL = 2048
HEAD_DIM = 128
N_HEADS_A = 8
N_KV_A = 2
GROUP_A = N_HEADS_A // N_KV_A
N_HEADS_B = 8
WIDTH_A = N_HEADS_A * HEAD_DIM
WIDTH_B = N_HEADS_B * HEAD_DIM
KV_A = N_KV_A * HEAD_DIM
PROJ_OUT = WIDTH_A + 2 * KV_A + 3 * WIDTH_B
GRID_W = 64
AXIAL_THETA = 10000.0
ROPE_THETA = 500000.0
PARTIAL_ROT = HEAD_DIM // 4
DILATIONS = (1, 4, 16)
BAND = 64
D_FF = 5632
EPS = 1e-6
Q_SCALE = HEAD_DIM ** -0.5 * math.log2(math.e)

MIB = 1024 * 1024

TM_PROJ = 512
TQ_A = 128
TQ_B = 128
WIN_B = TQ_B + 2 * BAND
TILES_PER_ITER_B = 4
TM_OUT = 256
TM_FFN = 512
TF_FFN = 512


def _params(semantics, vmem_mib):
    return pltpu.CompilerParams(dimension_semantics=semantics, vmem_limit_bytes=vmem_mib * MIB)


def _rms_scale(y):
    return lax.rsqrt(jnp.mean(y * y, axis=-1, keepdims=True) + EPS)


def _rope_tables(seq):
    def angles(pos, dim, theta):
        inv = theta ** (-(jnp.arange(0, dim, 2, dtype=F32) / dim))
        return pos[:, None] * inv[None, :]

    half = HEAD_DIM // 2
    rows = seq // GRID_W
    row_pos = jnp.repeat(jnp.arange(rows, dtype=F32), GRID_W)
    col_pos = jnp.tile(jnp.arange(GRID_W, dtype=F32), rows)
    ar, ac = angles(row_pos, half, AXIAL_THETA), angles(col_pos, half, AXIAL_THETA)
    z = jnp.zeros_like(ar)
    cos_a = jnp.concatenate([jnp.cos(ar), jnp.cos(ar), jnp.cos(ac), jnp.cos(ac)], axis=-1)
    up_a = jnp.concatenate([-jnp.sin(ar), z, -jnp.sin(ac), z], axis=-1)
    dn_a = jnp.concatenate([z, jnp.sin(ar), z, jnp.sin(ac)], axis=-1)

    ab = angles(jnp.arange(seq, dtype=F32), PARTIAL_ROT, ROPE_THETA)
    hw = PARTIAL_ROT // 2
    one_rest = jnp.ones((seq, HEAD_DIM - PARTIAL_ROT), F32)
    cos_b = jnp.concatenate([jnp.cos(ab), jnp.cos(ab), one_rest], axis=-1)
    up_b = jnp.concatenate([-jnp.sin(ab), jnp.zeros((seq, HEAD_DIM - hw), F32)], axis=-1)
    dn_b = jnp.concatenate([jnp.zeros((seq, hw), F32), jnp.sin(ab),
                            jnp.zeros((seq, HEAD_DIM - PARTIAL_ROT), F32)], axis=-1)
    return cos_a, up_a, dn_a, cos_b, up_b, dn_b


def _rotate(t, cos, s_up, s_dn, shift):
    return (t * cos + pltpu.roll(t, HEAD_DIM - shift, 1) * s_up
            + pltpu.roll(t, shift, 1) * s_dn)


def _inproj_kernel(x_ref, gmix_ref, w_ref, gq_ref, gk_ref,
                   ca_ref, ua_ref, da_ref, cb_ref, ub_ref, db_ref,
                   qa_ref, ka_ref, va_ref, qb_ref, kb_ref, vb_ref):
    x = x_ref[...]
    h = (x * _rms_scale(x) * gmix_ref[...]).astype(BF16)
    ca, ua, da = ca_ref[...], ua_ref[...], da_ref[...]
    cb, ub, db = cb_ref[...], ub_ref[...], db_ref[...]
    gq, gk = gq_ref[...], gk_ref[...]
    ones = jnp.ones((x.shape[0], HEAD_DIM), BF16)

    pair = {}

    def head(col):
        base = col - col % (2 * HEAD_DIM)
        if base not in pair:
            pair.clear()
            pair[base] = jnp.dot(h, w_ref[:, base:base + 2 * HEAD_DIM], preferred_element_type=F32)
        return pair[base][:, col - base:col - base + HEAD_DIM]

    def norm_rope_a(y, g, scale):
        return _rotate(y * g, ca, ua, da, HEAD_DIM // 4) * (_rms_scale(y) * scale)

    col = 0
    for i in range(N_HEADS_A):
        qa_ref[:, i * HEAD_DIM:(i + 1) * HEAD_DIM] = norm_rope_a(head(col), gq, Q_SCALE).astype(BF16)
        col += HEAD_DIM
    for i in range(N_KV_A):
        ka_ref[i] = norm_rope_a(head(col), gk, 1.0).astype(BF16)
        col += HEAD_DIM
    for i in range(N_KV_A):
        va_ref[i, :, :HEAD_DIM] = head(col).astype(BF16)
        va_ref[i, :, HEAD_DIM:] = ones
        col += HEAD_DIM
    for i in range(N_HEADS_B):
        qb_ref[i] = _rotate(head(col), cb, ub, db, PARTIAL_ROT // 2) * Q_SCALE
        col += HEAD_DIM
    for i in range(N_HEADS_B):
        kb_ref[i] = _rotate(head(col), cb, ub, db, PARTIAL_ROT // 2)
        col += HEAD_DIM
    for i in range(N_HEADS_B):
        vb_ref[i] = head(col)
        col += HEAD_DIM


def _in_projection(x2, g_mix, w_in, g_q, g_k, tables, seq):
    m = x2.shape[0]
    tm = TM_PROJ
    steps_per_seq = seq // tm
    row = lambda i: (i, 0)
    fixed = lambda i: (0, 0)
    tab = pl.BlockSpec((tm, HEAD_DIM), lambda i: (i % steps_per_seq, 0))
    heads = lambda n: pl.BlockSpec((n, tm, HEAD_DIM), lambda i: (0, i, 0))
    return pl.pallas_call(
        _inproj_kernel,
        grid=(m // tm,),
        in_specs=[
            pl.BlockSpec((tm, D_MODEL), row),
            pl.BlockSpec((1, D_MODEL), fixed),
            pl.BlockSpec((D_MODEL, PROJ_OUT), fixed, pipeline_mode=pl.Buffered(1)),
            pl.BlockSpec((1, HEAD_DIM), fixed),
            pl.BlockSpec((1, HEAD_DIM), fixed),
            tab, tab, tab, tab, tab, tab,
        ],
        out_specs=[
            pl.BlockSpec((tm, WIDTH_A), row),
            heads(N_KV_A),
            pl.BlockSpec((N_KV_A, tm, 2 * HEAD_DIM), lambda i: (0, i, 0)),
            heads(N_HEADS_B), heads(N_HEADS_B), heads(N_HEADS_B),
        ],
        out_shape=[
            jax.ShapeDtypeStruct((m, WIDTH_A), BF16),
            jax.ShapeDtypeStruct((N_KV_A, m, HEAD_DIM), BF16),
            jax.ShapeDtypeStruct((N_KV_A, m, 2 * HEAD_DIM), BF16),
            jax.ShapeDtypeStruct((N_HEADS_B, m, HEAD_DIM), F32),
            jax.ShapeDtypeStruct((N_HEADS_B, m, HEAD_DIM), F32),
            jax.ShapeDtypeStruct((N_HEADS_B, m, HEAD_DIM), F32),
        ],
        compiler_params=_params(("parallel",), 56),
        name="in_projection",
    )(x2, g_mix, w_in, g_q, g_k, *tables)


def _mixer_a_kernel(q_ref, k_ref, v_ref, o_ref):
    k = k_ref[0]
    v = v_ref[0]
    for h in range(GROUP_A):
        q = q_ref[:, h * HEAD_DIM:(h + 1) * HEAD_DIM]
        s = lax.dot_general(q, k, (((1,), (1,)), ((), ())), preferred_element_type=F32)
        p = jnp.exp2(s - jnp.max(s, axis=-1, keepdims=True)).astype(BF16)
        o = jnp.dot(p, v, preferred_element_type=F32)
        o_ref[:, h * HEAD_DIM:(h + 1) * HEAD_DIM] = o[:, :HEAD_DIM] / o[:, HEAD_DIM:]


def _mixer_a(qa, ka, va, batch, seq):
    m = qa.shape[0]
    tq = TQ_A
    qsteps = seq // tq
    gw = GROUP_A * HEAD_DIM
    return pl.pallas_call(
        _mixer_a_kernel,
        grid=(batch, N_KV_A, qsteps),
        in_specs=[
            pl.BlockSpec((tq, gw), lambda b, g, i: (b * qsteps + i, g)),
            pl.BlockSpec((1, seq, HEAD_DIM), lambda b, g, i: (g, b, 0)),
            pl.BlockSpec((1, seq, 2 * HEAD_DIM), lambda b, g, i: (g, b, 0)),
        ],
        out_specs=pl.BlockSpec((tq, gw), lambda b, g, i: (b * qsteps + i, g)),
        out_shape=jax.ShapeDtypeStruct((m, WIDTH_A), F32),
        compiler_params=_params(("parallel", "parallel", "arbitrary"), 48),
        name="mixer_a",
    )(qa, ka, va)


def _mixer_b_kernel(q_ref, k_ref, v_ref, o_ref, num_ref, den_ref, m_ref):
    seq = q_ref.shape[0]
    ones = jnp.ones((WIN_B, HEAD_DIM), BF16)
    dq = lax.broadcasted_iota(jnp.int32, (TQ_B, WIN_B), 0)
    dk = lax.broadcasted_iota(jnp.int32, (TQ_B, WIN_B), 1)

    def rows(r, l, size, d):
        if d == 1:
            return pl.ds(pl.multiple_of(l, BAND), size)
        return pl.ds(r + d * l, size, stride=d)

    for bi, d in enumerate(DILATIONS):
        length = seq // d
        tiles_per_class = length // TQ_B

        def tile(j, bi=bi, d=d, length=length, tiles_per_class=tiles_per_class):
            r = j // tiles_per_class
            l0 = (j % tiles_per_class) * TQ_B
            ws = jnp.clip(l0 - BAND, 0, length - WIN_B)
            q_rows = rows(r, l0, TQ_B, d)
            k_rows = rows(r, ws, WIN_B, d)
            q = q_ref[q_rows, :].astype(BF16)
            k = k_ref[k_rows, :].astype(BF16)
            v = jnp.concatenate([v_ref[k_rows, :].astype(BF16), ones], axis=1)
            s = lax.dot_general(q, k, (((1,), (1,)), ((), ())), preferred_element_type=F32)
            valid = jnp.abs((l0 + dq) - (ws + dk)) <= BAND
            s = jnp.where(valid, s, -jnp.inf)
            mx = jnp.max(s, axis=-1, keepdims=True)
            p = jnp.exp2(s - mx).astype(BF16)
            pv = jnp.dot(p, v, preferred_element_type=F32)
            num_ref[bi, q_rows, :] = pv[:, :HEAD_DIM]
            den_ref[bi, q_rows, :] = pv[:, HEAD_DIM:]
            m_ref[bi, q_rows, :] = jnp.broadcast_to(mx, (TQ_B, HEAD_DIM))

        def body(it, carry, tile=tile):
            for n in range(TILES_PER_ITER_B):
                tile(it * TILES_PER_ITER_B + n)
            return carry

        lax.fori_loop(0, (seq // TQ_B) // TILES_PER_ITER_B, body, 0)

    m0, m1, m2 = m_ref[0], m_ref[1], m_ref[2]
    mx = jnp.maximum(jnp.maximum(m0, m1), m2)
    num = jnp.zeros((seq, HEAD_DIM), F32)
    den = jnp.zeros((seq, HEAD_DIM), F32)
    for bi, mb in enumerate((m0, m1, m2)):
        w = jnp.exp2(mb - mx)
        num = num + w * num_ref[bi]
        den = den + w * den_ref[bi]
    o_ref[...] = num / den


def _mixer_b(qb, kb, vb, batch, seq):
    m = qb.shape[1]
    spec = pl.BlockSpec((None, seq, HEAD_DIM), lambda b, h: (h, b, 0))
    return pl.pallas_call(
        _mixer_b_kernel,
        grid=(batch, N_HEADS_B),
        in_specs=[spec, spec, spec],
        out_specs=pl.BlockSpec((seq, HEAD_DIM), lambda b, h: (b, h)),
        out_shape=jax.ShapeDtypeStruct((m, WIDTH_B), F32),
        scratch_shapes=[
            pltpu.VMEM((len(DILATIONS), seq, HEAD_DIM), F32),
            pltpu.VMEM((len(DILATIONS), seq, HEAD_DIM), F32),
            pltpu.VMEM((len(DILATIONS), seq, HEAD_DIM), F32),
        ],
        compiler_params=_params(("parallel", "parallel"), 56),
        name="mixer_b",
    )(qb, kb, vb)


def _outproj_kernel(oa_ref, ob_ref, x_ref, ga_ref, gb_ref, w_ref, gffn_ref, x1_ref, h2_ref):
    oa, ob = oa_ref[...], ob_ref[...]
    na = (oa * _rms_scale(oa) * ga_ref[...]).astype(BF16)
    nb = (ob * _rms_scale(ob) * gb_ref[...]).astype(BF16)
    x1 = (x_ref[...]
          + jnp.dot(na, w_ref[:WIDTH_A, :], preferred_element_type=F32)
          + jnp.dot(nb, w_ref[WIDTH_A:, :], preferred_element_type=F32))
    x1_ref[...] = x1
    h2_ref[...] = (x1 * _rms_scale(x1) * gffn_ref[...]).astype(BF16)


def _out_projection(oa, ob, x2, g_a, g_b, w_out, g_ffn):
    m = x2.shape[0]
    tm = TM_OUT
    row = lambda i: (i, 0)
    fixed = lambda i: (0, 0)
    return pl.pallas_call(
        _outproj_kernel,
        grid=(m // tm,),
        in_specs=[
            pl.BlockSpec((tm, WIDTH_A), row),
            pl.BlockSpec((tm, WIDTH_B), row),
            pl.BlockSpec((tm, D_MODEL), row),
            pl.BlockSpec((1, WIDTH_A), fixed),
            pl.BlockSpec((1, WIDTH_B), fixed),
            pl.BlockSpec((WIDTH_A + WIDTH_B, D_MODEL), fixed, pipeline_mode=pl.Buffered(1)),
            pl.BlockSpec((1, D_MODEL), fixed),
        ],
        out_specs=[pl.BlockSpec((tm, D_MODEL), row), pl.BlockSpec((tm, D_MODEL), row)],
        out_shape=[jax.ShapeDtypeStruct((m, D_MODEL), F32), jax.ShapeDtypeStruct((m, D_MODEL), BF16)],
        compiler_params=_params(("parallel",), 40),
        name="out_projection",
    )(oa, ob, x2, g_a, g_b, w_out, g_ffn)


def _ffn_kernel(h_ref, x1_ref, wg_ref, wu_ref, wd_ref, gfin_ref, o_ref, acc_ref):
    f = pl.program_id(1)
    h = h_ref[...]
    gate = jnp.dot(h, wg_ref[...], preferred_element_type=F32)
    up = jnp.dot(h, wu_ref[...], preferred_element_type=F32)
    act = (gate * jax.nn.sigmoid(gate) * up).astype(BF16)
    part = jnp.dot(act, wd_ref[...], preferred_element_type=F32)

    @pl.when(f == 0)
    def _():
        acc_ref[...] = part

    @pl.when(f > 0)
    def _():
        acc_ref[...] += part

    @pl.when(f == pl.num_programs(1) - 1)
    def _():
        y = x1_ref[...] + acc_ref[...]
        o_ref[...] = y * _rms_scale(y) * gfin_ref[...]


def _ffn(h2, x1, w_gate_up, w_down, g_final):
    m = h2.shape[0]
    tm, tf = TM_FFN, TF_FFN
    nf = D_FF // tf
    return pl.pallas_call(
        _ffn_kernel,
        grid=(m // tm, nf),
        in_specs=[
            pl.BlockSpec((tm, D_MODEL), lambda i, f: (i, 0)),
            pl.BlockSpec((tm, D_MODEL), lambda i, f: (i, 0)),
            pl.BlockSpec((D_MODEL, tf), lambda i, f: (0, f)),
            pl.BlockSpec((D_MODEL, tf), lambda i, f: (0, f + nf)),
            pl.BlockSpec((tf, D_MODEL), lambda i, f: (f, 0)),
            pl.BlockSpec((1, D_MODEL), lambda i, f: (0, 0)),
        ],
        out_specs=pl.BlockSpec((tm, D_MODEL), lambda i, f: (i, 0)),
        out_shape=jax.ShapeDtypeStruct((m, D_MODEL), F32),
        scratch_shapes=[pltpu.VMEM((tm, D_MODEL), F32)],
        compiler_params=_params(("parallel", "arbitrary"), 48),
        name="ffn",
    )(h2, x1, w_gate_up, w_gate_up, w_down, g_final)


def kernel(x, g_mix, w_in, g_q_a, g_k_a, g_out_a, g_out_b, w_out, g_ffn, w_gate_up, w_down, g_final):
    batch, seq, d = x.shape
    assert d == D_MODEL and seq % GRID_W == 0
    assert seq % TM_PROJ == 0 and seq % (TQ_B * max(DILATIONS)) == 0 and seq // max(DILATIONS) >= WIN_B
    depth = w_in.shape[0]
    tables = _rope_tables(seq)
    x2 = x.reshape(batch * seq, d)
    for layer in range(depth):
        qa, ka, va, qb, kb, vb = _in_projection(
            x2, g_mix[layer][None], w_in[layer].astype(BF16),
            g_q_a[layer][None], g_k_a[layer][None], tables, seq)
        oa = _mixer_a(qa, ka, va, batch, seq)
        ob = _mixer_b(qb, kb, vb, batch, seq)
        x1, h2 = _out_projection(oa, ob, x2, g_out_a[layer][None], g_out_b[layer][None],
                                 w_out[layer].astype(BF16), g_ffn[layer][None])
        assert depth == 1
        x2 = _ffn(h2, x1, w_gate_up[layer].astype(BF16), w_down[layer].astype(BF16), g_final[None])
    return x2.reshape(batch, seq, d)
```

```python
import functools
import math

import jax
import jax.numpy as jnp
from jax import lax
from jax.experimental import pallas as pl
from jax.experimental.pallas import tpu as pltpu

F32 = jnp.float32
BF16 = jnp.bfloat16

D_MODEL = 2048
HEAD_DIM = 128
N_HEADS_A = 8
N_KV_A = 2
GROUP_A = N_HEADS_A // N_KV_A
N_HEADS_B = 8
WIDTH_A = N_HEADS_A * HEAD_DIM
WIDTH_B = N_HEADS_B * HEAD_DIM
KV_A = N_KV_A * HEAD_DIM
PROJ_OUT = WIDTH_A + 2 * KV_A + 3 * WIDTH_B
GRID_W = 64
AXIAL_THETA = 10000.0
ROPE_THETA = 500000.0
PARTIAL_ROT = HEAD_DIM // 4
DILATIONS = (1, 4, 16)
BAND = 64
D_FF = 5632
EPS = 1e-6
Q_SCALE = HEAD_DIM ** -0.5 * math.log2(math.e)

MIB = 1024 * 1024

TM_PROJ = 512
TQ_A = 256
TK_A = 512
TQ_B = 128
WIN_B = TQ_B + 2 * BAND
TILES_PER_ITER_B = 4
TM_OUT = 256
TM_FFN = 512
TF_FFN = 512


def _params(semantics, vmem_mib):
    return pltpu.CompilerParams(dimension_semantics=semantics, vmem_limit_bytes=vmem_mib * MIB)


def _rms_scale(y):
    return lax.rsqrt(jnp.mean(y * y, axis=-1, keepdims=True) + EPS)


def _rope_tables(seq):
    def angles(pos, dim, theta):
        inv = theta ** (-(jnp.arange(0, dim, 2, dtype=F32) / dim))
        return pos[:, None] * inv[None, :]

    half = HEAD_DIM // 2
    rows = seq // GRID_W
    row_pos = jnp.repeat(jnp.arange(rows, dtype=F32), GRID_W)
    col_pos = jnp.tile(jnp.arange(GRID_W, dtype=F32), rows)
    ar, ac = angles(row_pos, half, AXIAL_THETA), angles(col_pos, half, AXIAL_THETA)
    z = jnp.zeros_like(ar)
    cos_a = jnp.concatenate([jnp.cos(ar), jnp.cos(ar), jnp.cos(ac), jnp.cos(ac)], axis=-1)
    up_a = jnp.concatenate([-jnp.sin(ar), z, -jnp.sin(ac), z], axis=-1)
    dn_a = jnp.concatenate([z, jnp.sin(ar), z, jnp.sin(ac)], axis=-1)

    ab = angles(jnp.arange(seq, dtype=F32), PARTIAL_ROT, ROPE_THETA)
    hw = PARTIAL_ROT // 2
    one_rest = jnp.ones((seq, HEAD_DIM - PARTIAL_ROT), F32)
    cos_b = jnp.concatenate([jnp.cos(ab), jnp.cos(ab), one_rest], axis=-1)
    up_b = jnp.concatenate([-jnp.sin(ab), jnp.zeros((seq, HEAD_DIM - hw), F32)], axis=-1)
    dn_b = jnp.concatenate([jnp.zeros((seq, hw), F32), jnp.sin(ab),
                            jnp.zeros((seq, HEAD_DIM - PARTIAL_ROT), F32)], axis=-1)
    return cos_a, up_a, dn_a, cos_b, up_b, dn_b


def _rotate(t, cos, s_up, s_dn, shift):
    return (t * cos + pltpu.roll(t, HEAD_DIM - shift, 1) * s_up
            + pltpu.roll(t, shift, 1) * s_dn)


def _inproj_kernel(x_ref, gmix_ref, w_ref, gq_ref, gk_ref,
                   ca_ref, ua_ref, da_ref, cb_ref, ub_ref, db_ref,
                   qa_ref, ka_ref, va_ref, qb_ref, kb_ref, vb_ref):
    x = x_ref[...]
    h = (x * _rms_scale(x) * gmix_ref[...]).astype(BF16)
    ca, ua, da = ca_ref[...], ua_ref[...], da_ref[...]
    cb, ub, db = cb_ref[...], ub_ref[...], db_ref[...]
    gq, gk = gq_ref[...], gk_ref[...]
    ones = jnp.ones((x.shape[0], HEAD_DIM), BF16)

    pair = {}

    def head(col):
        base = col - col % (2 * HEAD_DIM)
        if base not in pair:
            pair.clear()
            pair[base] = jnp.dot(h, w_ref[:, base:base + 2 * HEAD_DIM], preferred_element_type=F32)
        return pair[base][:, col - base:col - base + HEAD_DIM]

    def norm_rope_a(y, g, scale):
        return _rotate(y * g, ca, ua, da, HEAD_DIM // 4) * (_rms_scale(y) * scale)

    col = 0
    for i in range(N_HEADS_A):
        qa_ref[i] = norm_rope_a(head(col), gq, Q_SCALE).astype(BF16)
        col += HEAD_DIM
    for i in range(N_KV_A):
        ka_ref[i] = norm_rope_a(head(col), gk, 1.0).astype(BF16)
        col += HEAD_DIM
    for i in range(N_KV_A):
        va_ref[i, :, :HEAD_DIM] = head(col).astype(BF16)
        va_ref[i, :, HEAD_DIM:] = ones
        col += HEAD_DIM
    for i in range(N_HEADS_B):
        qb_ref[i] = _rotate(head(col), cb, ub, db, PARTIAL_ROT // 2) * Q_SCALE
        col += HEAD_DIM
    for i in range(N_HEADS_B):
        kb_ref[i] = _rotate(head(col), cb, ub, db, PARTIAL_ROT // 2)
        col += HEAD_DIM
    for i in range(N_HEADS_B):
        vb_ref[i] = head(col)
        col += HEAD_DIM


def _in_projection(x2, g_mix, w_in, g_q, g_k, tables, seq):
    m = x2.shape[0]
    tm = TM_PROJ
    steps_per_seq = seq // tm
    row = lambda i: (i, 0)
    fixed = lambda i: (0, 0)
    tab = pl.BlockSpec((tm, HEAD_DIM), lambda i: (i % steps_per_seq, 0))
    heads = lambda n: pl.BlockSpec((n, tm, HEAD_DIM), lambda i: (0, i, 0))
    return pl.pallas_call(
        _inproj_kernel,
        grid=(m // tm,),
        in_specs=[
            pl.BlockSpec((tm, D_MODEL), row),
            pl.BlockSpec((1, D_MODEL), fixed),
            pl.BlockSpec((D_MODEL, PROJ_OUT), fixed, pipeline_mode=pl.Buffered(1)),
            pl.BlockSpec((1, HEAD_DIM), fixed),
            pl.BlockSpec((1, HEAD_DIM), fixed),
            tab, tab, tab, tab, tab, tab,
        ],
        out_specs=[
            heads(N_HEADS_A),
            heads(N_KV_A),
            pl.BlockSpec((N_KV_A, tm, 2 * HEAD_DIM), lambda i: (0, i, 0)),
            heads(N_HEADS_B), heads(N_HEADS_B), heads(N_HEADS_B),
        ],
        out_shape=[
            jax.ShapeDtypeStruct((N_HEADS_A, m, HEAD_DIM), BF16),
            jax.ShapeDtypeStruct((N_KV_A, m, HEAD_DIM), BF16),
            jax.ShapeDtypeStruct((N_KV_A, m, 2 * HEAD_DIM), BF16),
            jax.ShapeDtypeStruct((N_HEADS_B, m, HEAD_DIM), F32),
            jax.ShapeDtypeStruct((N_HEADS_B, m, HEAD_DIM), F32),
            jax.ShapeDtypeStruct((N_HEADS_B, m, HEAD_DIM), F32),
        ],
        compiler_params=_params(("parallel",), 56),
        name="in_projection",
    )(x2, g_mix, w_in, g_q, g_k, *tables)


def _mixer_a_kernel(q_ref, k_ref, v_ref, o_ref, s_ref, m_ref):
    tq = q_ref.shape[1]
    seq = k_ref.shape[1]

    @pl.when(pl.program_id(0) == 0)
    def _():
        s_ref[...] = jnp.zeros(s_ref.shape, F32)
        m_ref[...] = jnp.zeros(m_ref.shape, F32)

    q = jnp.concatenate([q_ref[h] for h in range(GROUP_A)], axis=0)
    m_done = m_ref[:, :1]
    acc = None
    m_run = None
    for c in range(0, seq, TK_A):
        p = jnp.exp2(s_ref[:, c:c + TK_A] - m_done).astype(BF16)
        pv = jnp.dot(p, v_ref[0, c:c + TK_A, :], preferred_element_type=F32)
        acc = pv if acc is None else acc + pv
        s = lax.dot_general(q, k_ref[0, c:c + TK_A, :], (((1,), (1,)), ((), ())),
                            preferred_element_type=F32)
        s_ref[:, c:c + TK_A] = s
        for l in range(0, TK_A, HEAD_DIM):
            part = s[:, l:l + HEAD_DIM]
            m_run = part if m_run is None else jnp.maximum(m_run, part)
    m_ref[...] = jnp.broadcast_to(jnp.max(m_run, axis=-1, keepdims=True), m_ref.shape)
    o = acc[:, :HEAD_DIM] / acc[:, HEAD_DIM:]
    for h in range(GROUP_A):
        o_ref[:, h * HEAD_DIM:(h + 1) * HEAD_DIM] = o[h * tq:(h + 1) * tq]


def _mixer_a(qa, ka, va, batch, seq):
    m = qa.shape[1]
    tq = TQ_A
    tiles = seq // tq
    gw = GROUP_A * HEAD_DIM
    n = batch * N_KV_A * tiles

    def coords(t):
        bg, i = t // tiles, t % tiles
        b, g = bg // N_KV_A, bg % N_KV_A
        return b * tiles + i, g, b

    def started(j):
        return coords(jnp.minimum(j, n - 1))

    def finished(j):
        return coords(jnp.maximum(j - 1, 0))

    return pl.pallas_call(
        _mixer_a_kernel,
        grid=(n + 1,),
        in_specs=[
            pl.BlockSpec((GROUP_A, tq, HEAD_DIM), lambda j: (started(j)[1], started(j)[0], 0)),
            pl.BlockSpec((1, seq, HEAD_DIM), lambda j: (started(j)[1], started(j)[2], 0)),
            pl.BlockSpec((1, seq, 2 * HEAD_DIM), lambda j: (finished(j)[1], finished(j)[2], 0)),
        ],
        out_specs=pl.BlockSpec((tq, gw), lambda j: finished(j)[:2]),
        out_shape=jax.ShapeDtypeStruct((m, WIDTH_A), F32),
        scratch_shapes=[pltpu.VMEM((GROUP_A * tq, seq), F32),
                        pltpu.VMEM((GROUP_A * tq, HEAD_DIM), F32)],
        compiler_params=_params(("arbitrary",), 48),
        name="mixer_a",
    )(qa, ka, va)


def _mixer_b_kernel(q_ref, k_ref, v_ref, o_ref, num_ref, den_ref, m_ref):
    seq = q_ref.shape[0]
    ones = jnp.ones((WIN_B, HEAD_DIM), BF16)
    dq = lax.broadcasted_iota(jnp.int32, (TQ_B, WIN_B), 0)
    dk = lax.broadcasted_iota(jnp.int32, (TQ_B, WIN_B), 1)

    def rows(r, l, size, d):
        if d == 1:
            return pl.ds(pl.multiple_of(l, BAND), size)
        return pl.ds(r + d * l, size, stride=d)

    for bi, d in enumerate(DILATIONS):
        length = seq // d
        tiles_per_class = length // TQ_B

        def tile(j, bi=bi, d=d, length=length, tiles_per_class=tiles_per_class):
            r = j // tiles_per_class
            l0 = (j % tiles_per_class) * TQ_B
            ws = jnp.clip(l0 - BAND, 0, length - WIN_B)
            q_rows = rows(r, l0, TQ_B, d)
            k_rows = rows(r, ws, WIN_B, d)
            q = q_ref[q_rows, :].astype(BF16)
            k = k_ref[k_rows, :].astype(BF16)
            v = jnp.concatenate([v_ref[k_rows, :].astype(BF16), ones], axis=1)
            s = lax.dot_general(q, k, (((1,), (1,)), ((), ())), preferred_element_type=F32)
            valid = jnp.abs((l0 + dq) - (ws + dk)) <= BAND
            s = jnp.where(valid, s, -jnp.inf)
            mx = jnp.max(s, axis=-1, keepdims=True)
            p = jnp.exp2(s - mx).astype(BF16)
            pv = jnp.dot(p, v, preferred_element_type=F32)
            num_ref[bi, q_rows, :] = pv[:, :HEAD_DIM]
            den_ref[bi, q_rows, :] = pv[:, HEAD_DIM:]
            m_ref[bi, q_rows, :] = jnp.broadcast_to(mx, (TQ_B, HEAD_DIM))

        def body(it, carry, tile=tile):
            for n in range(TILES_PER_ITER_B):
                tile(it * TILES_PER_ITER_B + n)
            return carry

        lax.fori_loop(0, (seq // TQ_B) // TILES_PER_ITER_B, body, 0)

    m0, m1, m2 = m_ref[0], m_ref[1], m_ref[2]
    mx = jnp.maximum(jnp.maximum(m0, m1), m2)
    num = jnp.zeros((seq, HEAD_DIM), F32)
    den = jnp.zeros((seq, HEAD_DIM), F32)
    for bi, mb in enumerate((m0, m1, m2)):
        w = jnp.exp2(mb - mx)
        num = num + w * num_ref[bi]
        den = den + w * den_ref[bi]
    o_ref[...] = num / den


def _mixer_b(qb, kb, vb, batch, seq):
    m = qb.shape[1]
    spec = pl.BlockSpec((None, seq, HEAD_DIM), lambda b, h: (h, b, 0))
    return pl.pallas_call(
        _mixer_b_kernel,
        grid=(batch, N_HEADS_B),
        in_specs=[spec, spec, spec],
        out_specs=pl.BlockSpec((seq, HEAD_DIM), lambda b, h: (b, h)),
        out_shape=jax.ShapeDtypeStruct((m, WIDTH_B), F32),
        scratch_shapes=[
            pltpu.VMEM((len(DILATIONS), seq, HEAD_DIM), F32),
            pltpu.VMEM((len(DILATIONS), seq, HEAD_DIM), F32),
            pltpu.VMEM((len(DILATIONS), seq, HEAD_DIM), F32),
        ],
        compiler_params=_params(("parallel", "parallel"), 56),
        name="mixer_b",
    )(qb, kb, vb)


def _outproj_kernel(oa_ref, ob_ref, x_ref, ga_ref, gb_ref, w_ref, gffn_ref, x1_ref, h2_ref):
    oa, ob = oa_ref[...], ob_ref[...]
    na = (oa * _rms_scale(oa) * ga_ref[...]).astype(BF16)
    nb = (ob * _rms_scale(ob) * gb_ref[...]).astype(BF16)
    x1 = (x_ref[...]
          + jnp.dot(na, w_ref[:WIDTH_A, :], preferred_element_type=F32)
          + jnp.dot(nb, w_ref[WIDTH_A:, :], preferred_element_type=F32))
    x1_ref[...] = x1
    h2_ref[...] = (x1 * _rms_scale(x1) * gffn_ref[...]).astype(BF16)


def _out_projection(oa, ob, x2, g_a, g_b, w_out, g_ffn):
    m = x2.shape[0]
    tm = TM_OUT
    row = lambda i: (i, 0)
    fixed = lambda i: (0, 0)
    return pl.pallas_call(
        _outproj_kernel,
        grid=(m // tm,),
        in_specs=[
            pl.BlockSpec((tm, WIDTH_A), row),
            pl.BlockSpec((tm, WIDTH_B), row),
            pl.BlockSpec((tm, D_MODEL), row),
            pl.BlockSpec((1, WIDTH_A), fixed),
            pl.BlockSpec((1, WIDTH_B), fixed),
            pl.BlockSpec((WIDTH_A + WIDTH_B, D_MODEL), fixed, pipeline_mode=pl.Buffered(1)),
            pl.BlockSpec((1, D_MODEL), fixed),
        ],
        out_specs=[pl.BlockSpec((tm, D_MODEL), row), pl.BlockSpec((tm, D_MODEL), row)],
        out_shape=[jax.ShapeDtypeStruct((m, D_MODEL), F32), jax.ShapeDtypeStruct((m, D_MODEL), BF16)],
        compiler_params=_params(("parallel",), 40),
        name="out_projection",
    )(oa, ob, x2, g_a, g_b, w_out, g_ffn)


def _ffn_kernel(nf, h_ref, x1_ref, wg_ref, wu_ref, wd_ref, gfin_ref, o_ref, acc_ref, act_ref):
    j = pl.program_id(0)

    @pl.when(j == 0)
    def _():
        act_ref[...] = jnp.zeros(act_ref.shape, BF16)

    f_done = jnp.maximum(j - 1, 0) % nf
    part = jnp.dot(act_ref[...], wd_ref[...], preferred_element_type=F32)
    acc_ref[...] = jnp.where(f_done == 0, part, acc_ref[...] + part)

    h = h_ref[...]
    gate = jnp.dot(h, wg_ref[...], preferred_element_type=F32)
    up = jnp.dot(h, wu_ref[...], preferred_element_type=F32)
    act_ref[...] = (gate * jax.nn.sigmoid(gate) * up).astype(BF16)

    @pl.when(jnp.logical_and(j > 0, f_done == nf - 1))
    def _():
        y = x1_ref[...] + acc_ref[...]
        o_ref[...] = y * _rms_scale(y) * gfin_ref[...]


def _ffn(h2, x1, w_gate_up, w_down, g_final):
    m = h2.shape[0]
    tm, tf = TM_FFN, TF_FFN
    nf = D_FF // tf
    n = (m // tm) * nf

    def started(j):
        t = jnp.minimum(j, n - 1)
        return t // nf, t % nf

    def finished(j):
        t = jnp.maximum(j - 1, 0)
        return t // nf, t % nf

    return pl.pallas_call(
        functools.partial(_ffn_kernel, nf),
        grid=(n + 1,),
        in_specs=[
            pl.BlockSpec((tm, D_MODEL), lambda j: (started(j)[0], 0)),
            pl.BlockSpec((tm, D_MODEL), lambda j: (finished(j)[0], 0)),
            pl.BlockSpec((D_MODEL, tf), lambda j: (0, started(j)[1])),
            pl.BlockSpec((D_MODEL, tf), lambda j: (0, started(j)[1] + nf)),
            pl.BlockSpec((tf, D_MODEL), lambda j: (finished(j)[1], 0)),
            pl.BlockSpec((1, D_MODEL), lambda j: (0, 0)),
        ],
        out_specs=pl.BlockSpec((tm, D_MODEL), lambda j: (finished(j)[0], 0)),
        out_shape=jax.ShapeDtypeStruct((m, D_MODEL), F32),
        scratch_shapes=[pltpu.VMEM((tm, D_MODEL), F32), pltpu.VMEM((tm, tf), BF16)],
        compiler_params=_params(("arbitrary",), 48),
        name="ffn",
    )(h2, x1, w_gate_up, w_gate_up, w_down, g_final)


def kernel(x, g_mix, w_in, g_q_a, g_k_a, g_out_a, g_out_b, w_out, g_ffn, w_gate_up, w_down, g_final):
    batch, seq, d = x.shape
    assert d == D_MODEL and seq % GRID_W == 0
    assert seq % TM_PROJ == 0 and seq % (TQ_B * max(DILATIONS)) == 0 and seq // max(DILATIONS) >= WIN_B
    depth = w_in.shape[0]
    tables = _rope_tables(seq)
    x2 = x.reshape(batch * seq, d)
    for layer in range(depth):
        qa, ka, va, qb, kb, vb = _in_projection(
            x2, g_mix[layer][None], w_in[layer].astype(BF16),
            g_q_a[layer][None], g_k_a[layer][None], tables, seq)
        oa = _mixer_a(qa, ka, va, batch, seq)
        ob = _mixer_b(qb, kb, vb, batch, seq)
        x1, h2 = _out_projection(oa, ob, x2, g_out_a[layer][None], g_out_b[layer][None],
                                 w_out[layer].astype(BF16), g_ffn[layer][None])
        assert depth == 1
        x2 = _ffn(h2, x1, w_gate_up[layer].astype(BF16), w_down[layer].astype(BF16), g_final[None])
    return x2.reshape(batch, seq, d)
```

```python
import functools
import math

import jax
import jax.numpy as jnp
from jax import lax
from jax.experimental import pallas as pl
from jax.experimental.pallas import tpu as pltpu

F32 = jnp.float32
BF16 = jnp.bfloat16

D_MODEL = 2048
HEAD_DIM = 128
N_HEADS_A = 8
N_KV_A = 2
GROUP_A = N_HEADS_A // N_KV_A
N_HEADS_B = 8
WIDTH_A = N_HEADS_A * HEAD_DIM
WIDTH_B = N_HEADS_B * HEAD_DIM
KV_A = N_KV_A * HEAD_DIM
PROJ_OUT = WIDTH_A + 2 * KV_A + 3 * WIDTH_B
GRID_W = 64
AXIAL_THETA = 10000.0
ROPE_THETA = 500000.0
PARTIAL_ROT = HEAD_DIM // 4
DILATIONS = (1, 4, 16)
BAND = 64
D_FF = 5632
EPS = 1e-6
Q_SCALE = HEAD_DIM ** -0.5 * math.log2(math.e)

MIB = 1024 * 1024

TM_PROJ = 512
TQ_A = 256
TK_A = 512
TQ_B = 128
WIN_B = TQ_B + 2 * BAND
P4 = 4
GROUP_TILES_B = 8
TM_OUT = 256
TM_FFN = 512
TF_FFN = 512


def _params(semantics, vmem_mib):
    return pltpu.CompilerParams(dimension_semantics=semantics, vmem_limit_bytes=vmem_mib * MIB)


def _rms_scale(y):
    return lax.rsqrt(jnp.mean(y * y, axis=-1, keepdims=True) + EPS)


def _rope_tables(seq):
    def angles(pos, dim, theta):
        inv = theta ** (-(jnp.arange(0, dim, 2, dtype=F32) / dim))
        return pos[:, None] * inv[None, :]

    half = HEAD_DIM // 2
    rows = seq // GRID_W
    row_pos = jnp.repeat(jnp.arange(rows, dtype=F32), GRID_W)
    col_pos = jnp.tile(jnp.arange(GRID_W, dtype=F32), rows)
    ar, ac = angles(row_pos, half, AXIAL_THETA), angles(col_pos, half, AXIAL_THETA)
    z = jnp.zeros_like(ar)
    cos_a = jnp.concatenate([jnp.cos(ar), jnp.cos(ar), jnp.cos(ac), jnp.cos(ac)], axis=-1)
    up_a = jnp.concatenate([-jnp.sin(ar), z, -jnp.sin(ac), z], axis=-1)
    dn_a = jnp.concatenate([z, jnp.sin(ar), z, jnp.sin(ac)], axis=-1)

    ab = angles(jnp.arange(seq, dtype=F32), PARTIAL_ROT, ROPE_THETA)
    hw = PARTIAL_ROT // 2
    one_rest = jnp.ones((seq, HEAD_DIM - PARTIAL_ROT), F32)
    cos_b = jnp.concatenate([jnp.cos(ab), jnp.cos(ab), one_rest], axis=-1)
    up_b = jnp.concatenate([-jnp.sin(ab), jnp.zeros((seq, HEAD_DIM - hw), F32)], axis=-1)
    dn_b = jnp.concatenate([jnp.zeros((seq, hw), F32), jnp.sin(ab),
                            jnp.zeros((seq, HEAD_DIM - PARTIAL_ROT), F32)], axis=-1)
    return cos_a, up_a, dn_a, cos_b, up_b, dn_b


def _rotate(t, cos, s_up, s_dn, shift):
    return (t * cos + pltpu.roll(t, HEAD_DIM - shift, 1) * s_up
            + pltpu.roll(t, shift, 1) * s_dn)


def _inproj_kernel(x_ref, gmix_ref, w_ref, gq_ref, gk_ref,
                   ca_ref, ua_ref, da_ref, cb_ref, ub_ref, db_ref,
                   qa_ref, ka_ref, va_ref, qb_ref, kb_ref, vb_ref):
    x = x_ref[...]
    h = (x * _rms_scale(x) * gmix_ref[...]).astype(BF16)
    ca, ua, da = ca_ref[...], ua_ref[...], da_ref[...]
    cb, ub, db = cb_ref[...], ub_ref[...], db_ref[...]
    gq, gk = gq_ref[...], gk_ref[...]
    ones = jnp.ones((x.shape[0], HEAD_DIM), BF16)

    pair = {}
    head_sum = (lax.broadcasted_iota(jnp.int32, (2 * HEAD_DIM, 2 * HEAD_DIM), 0) // HEAD_DIM
                == lax.broadcasted_iota(jnp.int32, (2 * HEAD_DIM, 2 * HEAD_DIM), 1) // HEAD_DIM
                ).astype(BF16)

    def head(col):
        base = col - col % (2 * HEAD_DIM)
        if base not in pair:
            pair.clear()
            pair[base] = jnp.dot(h, w_ref[:, base:base + 2 * HEAD_DIM], preferred_element_type=F32)
        return pair[base][:, col - base:col - base + HEAD_DIM]

    qa0, ka0, va0 = 0, WIDTH_A, WIDTH_A + KV_A
    qb0 = WIDTH_A + 2 * KV_A
    kb0, vb0 = qb0 + WIDTH_B, qb0 + 2 * WIDTH_B
    for i in range(N_HEADS_B):
        qb_ref[i] = _rotate(head(qb0 + i * HEAD_DIM), cb, ub, db, PARTIAL_ROT // 2) * Q_SCALE
    for i in range(N_HEADS_B):
        kb_ref[i] = _rotate(head(kb0 + i * HEAD_DIM), cb, ub, db, PARTIAL_ROT // 2)

    ya = [head(c) for c in range(qa0, va0, HEAD_DIM)]
    z = jnp.concatenate([jnp.concatenate([ya[i] * ya[i], ya[i + 1] * ya[i + 1]], axis=1)
                         for i in range(0, len(ya), 2)], axis=0)
    z_hi = z.astype(BF16)
    z_lo = (z - z_hi.astype(F32)).astype(BF16)
    ssq = (jnp.dot(z_hi, head_sum, preferred_element_type=F32)
           + jnp.dot(z_lo, head_sum, preferred_element_type=F32))
    r = lax.rsqrt(ssq * (1.0 / HEAD_DIM) + EPS)
    tm = x.shape[0]
    for i, y in enumerate(ya):
        g, scale = (gq, Q_SCALE) if i < N_HEADS_A else (gk, 1.0)
        ri = r[(i // 2) * tm:(i // 2 + 1) * tm, (i % 2) * HEAD_DIM:(i % 2 + 1) * HEAD_DIM]
        out = (_rotate(y * g, ca, ua, da, HEAD_DIM // 4) * (ri * scale)).astype(BF16)
        if i < N_HEADS_A:
            qa_ref[i] = out
        else:
            ka_ref[i - N_HEADS_A] = out
    for i in range(N_KV_A):
        va_ref[i, :, :HEAD_DIM] = head(va0 + i * HEAD_DIM).astype(BF16)
        va_ref[i, :, HEAD_DIM:] = ones
    for i in range(N_HEADS_B):
        vb_ref[i] = head(vb0 + i * HEAD_DIM)


def _in_projection(x2, g_mix, w_in, g_q, g_k, tables, seq):
    m = x2.shape[0]
    tm = TM_PROJ
    steps_per_seq = seq // tm
    row = lambda i: (i, 0)
    fixed = lambda i: (0, 0)
    tab = pl.BlockSpec((tm, HEAD_DIM), lambda i: (i % steps_per_seq, 0))
    heads = lambda n: pl.BlockSpec((n, tm, HEAD_DIM), lambda i: (0, i, 0))
    return pl.pallas_call(
        _inproj_kernel,
        grid=(m // tm,),
        in_specs=[
            pl.BlockSpec((tm, D_MODEL), row),
            pl.BlockSpec((1, D_MODEL), fixed),
            pl.BlockSpec((D_MODEL, PROJ_OUT), fixed, pipeline_mode=pl.Buffered(1)),
            pl.BlockSpec((1, HEAD_DIM), fixed),
            pl.BlockSpec((1, HEAD_DIM), fixed),
            tab, tab, tab, tab, tab, tab,
        ],
        out_specs=[
            heads(N_HEADS_A),
            heads(N_KV_A),
            pl.BlockSpec((N_KV_A, tm, 2 * HEAD_DIM), lambda i: (0, i, 0)),
            heads(N_HEADS_B), heads(N_HEADS_B), heads(N_HEADS_B),
        ],
        out_shape=[
            jax.ShapeDtypeStruct((N_HEADS_A, m, HEAD_DIM), BF16),
            jax.ShapeDtypeStruct((N_KV_A, m, HEAD_DIM), BF16),
            jax.ShapeDtypeStruct((N_KV_A, m, 2 * HEAD_DIM), BF16),
            jax.ShapeDtypeStruct((N_HEADS_B, m, HEAD_DIM), F32),
            jax.ShapeDtypeStruct((N_HEADS_B, m, HEAD_DIM), F32),
            jax.ShapeDtypeStruct((N_HEADS_B, m, HEAD_DIM), F32),
        ],
        compiler_params=_params(("parallel",), 56),
        name="in_projection",
    )(x2, g_mix, w_in, g_q, g_k, *tables)


def _mixer_a_kernel(q_ref, k_ref, v_ref, o_ref, s_ref, m_ref):
    tq = q_ref.shape[1]
    seq = k_ref.shape[1]

    @pl.when(pl.program_id(0) == 0)
    def _():
        s_ref[...] = jnp.zeros(s_ref.shape, F32)
        m_ref[...] = jnp.zeros(m_ref.shape, F32)

    q = jnp.concatenate([q_ref[h] for h in range(GROUP_A)], axis=0)
    m_done = m_ref[:, :1]
    acc = None
    m_run = None
    for c in range(0, seq, TK_A):
        p = jnp.exp2(s_ref[:, c:c + TK_A] - m_done).astype(BF16)
        pv = jnp.dot(p, v_ref[0, c:c + TK_A, :], preferred_element_type=F32)
        acc = pv if acc is None else acc + pv
        s = lax.dot_general(q, k_ref[0, c:c + TK_A, :], (((1,), (1,)), ((), ())),
                            preferred_element_type=F32)
        s_ref[:, c:c + TK_A] = s
        for l in range(0, TK_A, HEAD_DIM):
            part = s[:, l:l + HEAD_DIM]
            m_run = part if m_run is None else jnp.maximum(m_run, part)
    m_ref[...] = jnp.broadcast_to(jnp.max(m_run, axis=-1, keepdims=True), m_ref.shape)
    o = acc[:, :HEAD_DIM] / acc[:, HEAD_DIM:]
    for h in range(GROUP_A):
        o_ref[:, h * HEAD_DIM:(h + 1) * HEAD_DIM] = o[h * tq:(h + 1) * tq]


def _mixer_a(qa, ka, va, batch, seq):
    m = qa.shape[1]
    tq = TQ_A
    tiles = seq // tq
    gw = GROUP_A * HEAD_DIM
    n = batch * N_KV_A * tiles

    def coords(t):
        bg, i = t // tiles, t % tiles
        b, g = bg // N_KV_A, bg % N_KV_A
        return b * tiles + i, g, b

    def started(j):
        return coords(jnp.minimum(j, n - 1))

    def finished(j):
        return coords(jnp.maximum(j - 1, 0))

    return pl.pallas_call(
        _mixer_a_kernel,
        grid=(n + 1,),
        in_specs=[
            pl.BlockSpec((GROUP_A, tq, HEAD_DIM), lambda j: (started(j)[1], started(j)[0], 0)),
            pl.BlockSpec((1, seq, HEAD_DIM), lambda j: (started(j)[1], started(j)[2], 0)),
            pl.BlockSpec((1, seq, 2 * HEAD_DIM), lambda j: (finished(j)[1], finished(j)[2], 0)),
        ],
        out_specs=pl.BlockSpec((tq, gw), lambda j: finished(j)[:2]),
        out_shape=jax.ShapeDtypeStruct((m, WIDTH_A), F32),
        scratch_shapes=[pltpu.VMEM((GROUP_A * tq, seq), F32),
                        pltpu.VMEM((GROUP_A * tq, HEAD_DIM), F32)],
        compiler_params=_params(("arbitrary",), 48),
        name="mixer_a",
    )(qa, ka, va)


def _mixer_b_bias(seq):
    i = jnp.arange(TQ_B)[:, None]
    j = jnp.arange(WIN_B)[None, :]
    q4, k4 = TQ_B // P4, WIN_B // P4
    kinds = [P4 * (off + i % q4 - j % k4) + i // q4 - j // k4 for off in (0, BAND // P4, 2 * BAND // P4)]
    kinds += [off + i - j for off in (0, BAND, 2 * BAND)]
    kinds += [off + i - j for off in (0, TQ_B)]
    return jnp.stack([jnp.where(jnp.abs(dist) <= BAND, 0.0, -jnp.inf).astype(F32) for dist in kinds])


def _mixer_b_kernel(q_ref, k_ref, v_ref, bias_ref, o_ref,
                    qf_ref, kf_ref, vf_ref, qb_ref, kb_ref, vb_ref, s_ref, m_ref, part_ref, lse_ref):
    seq = q_ref.shape[0]
    chunk = seq // P4
    g_tiles = s_ref.shape[0]
    n_groups = seq // TQ_B // g_tiles
    assert chunk == g_tiles * TQ_B and chunk == P4 * WIN_B

    vb_ref[:, HEAD_DIM:] = jnp.ones((seq, HEAD_DIM), BF16)
    for r in range(P4):
        dst = slice(r * chunk, (r + 1) * chunk)
        for src, f_ref, b_ref in ((q_ref, qf_ref, qb_ref), (k_ref, kf_ref, kb_ref), (v_ref, vf_ref, vb_ref)):
            x = src[pl.ds(r, chunk, stride=P4), :]
            f_ref[dst, :] = x
            b_ref[dst, :HEAD_DIM] = x.astype(BF16)

    def pieces(ref, start, rows):
        return jnp.concatenate([ref[pl.ds(r * chunk + start, rows), :] for r in range(P4)], axis=0)

    def tile_rows(branch, g, t):
        if branch == 0:
            q0 = pl.multiple_of(g * (g_tiles * TQ_B // P4) + t * (TQ_B // P4), TQ_B // P4)
            k0 = pl.multiple_of(jnp.clip(q0 - BAND // P4, 0, chunk - WIN_B // P4), BAND // P4)
            kind = (q0 - k0) // (BAND // P4)

            def put(ref, val):
                for r in range(P4):
                    ref[0, pl.ds(r * chunk + q0, TQ_B // P4), :] = val[r * (TQ_B // P4):(r + 1) * (TQ_B // P4)]
            return (lambda: pieces(qb_ref, q0, TQ_B // P4), lambda: pieces(kb_ref, k0, WIN_B // P4),
                    lambda: pieces(vb_ref, k0, WIN_B // P4), kind, put)
        base = pl.multiple_of(g * chunk, chunk)
        if branch == 1:
            q0 = t * TQ_B
            k0 = min(max(q0 - BAND, 0), chunk - WIN_B)
            kind = 3 + (q0 - k0) // BAND

            def put(ref, val):
                ref[1, pl.ds(base + q0, TQ_B), :] = val
            return (lambda: qb_ref[pl.ds(base + q0, TQ_B), :], lambda: kb_ref[pl.ds(base + k0, WIN_B), :],
                    lambda: vb_ref[pl.ds(base + k0, WIN_B), :], kind, put)
        a, u = t // 2, t % 2
        q_rows = pl.ds(base + a + P4 * TQ_B * u, TQ_B, stride=P4)
        k_rows = pl.ds(base + a, WIN_B, stride=P4)

        def put(ref, val):
            ref[2, q_rows, :] = val
        return (lambda: qf_ref[q_rows, :].astype(BF16), lambda: kf_ref[k_rows, :].astype(BF16),
                lambda: jnp.concatenate([vf_ref[k_rows, :].astype(BF16), jnp.ones((WIN_B, HEAD_DIM), BF16)], axis=1),
                6 + u, put)

    def start(branch, g, t):
        q, k, _, kind, _ = tile_rows(branch, g, t)
        s = lax.dot_general(q(), k(), (((1,), (1,)), ((), ())), preferred_element_type=F32) + bias_ref[kind]
        s_ref[t] = s
        m_ref[t] = jnp.broadcast_to(jnp.max(s, axis=-1, keepdims=True), (TQ_B, HEAD_DIM))

    def finish(branch, g, t):
        _, _, v, _, put = tile_rows(branch, g, t)
        m = m_ref[t]
        p = jnp.concatenate([jnp.exp2(s_ref[t, :, c:c + HEAD_DIM] - m) for c in range(0, WIN_B, HEAD_DIM)],
                            axis=1).astype(BF16)
        pv = jnp.dot(p, v(), preferred_element_type=F32)
        den = pv[:, HEAD_DIM:]
        put(part_ref, pv[:, :HEAD_DIM] / den)
        put(lse_ref, m + jnp.log2(den))

    for t in range(g_tiles):
        start(0, 0, t)
    for branch in range(len(DILATIONS)):
        def body(g, carry, branch=branch):
            for t in range(g_tiles):
                finish(branch, g - 1, t)
                start(branch, g, t)
            return carry

        lax.fori_loop(1, n_groups, body, 0)
        for t in range(g_tiles):
            finish(branch, n_groups - 1, t)
            if branch + 1 < len(DILATIONS):
                start(branch + 1, 0, t)

    for r in range(P4):
        rows = slice(r * chunk, (r + 1) * chunk)
        lse = [lse_ref[b, rows, :] for b in range(len(DILATIONS))]
        top = jnp.maximum(jnp.maximum(lse[0], lse[1]), lse[2])
        w = [jnp.exp2(l - top) for l in lse]
        num = w[0] * part_ref[0, rows, :] + w[1] * part_ref[1, rows, :] + w[2] * part_ref[2, rows, :]
        o_ref[pl.ds(r, chunk, stride=P4), :] = num / (w[0] + w[1] + w[2])


def _mixer_b(qb, kb, vb, batch, seq):
    m = qb.shape[1]
    spec = pl.BlockSpec((None, seq, HEAD_DIM), lambda b, h: (h, b, 0))
    bias = _mixer_b_bias(seq)
    f32_rows = pltpu.VMEM((seq, HEAD_DIM), F32)
    bf16_rows = pltpu.VMEM((seq, HEAD_DIM), BF16)
    per_branch = pltpu.VMEM((len(DILATIONS), seq, HEAD_DIM), F32)
    return pl.pallas_call(
        _mixer_b_kernel,
        grid=(batch, N_HEADS_B),
        in_specs=[spec, spec, spec, pl.BlockSpec(bias.shape, lambda b, h: (0, 0, 0))],
        out_specs=pl.BlockSpec((seq, HEAD_DIM), lambda b, h: (b, h)),
        out_shape=jax.ShapeDtypeStruct((m, WIDTH_B), F32),
        scratch_shapes=[
            f32_rows, f32_rows, f32_rows,
            bf16_rows, bf16_rows, pltpu.VMEM((seq, 2 * HEAD_DIM), BF16),
            pltpu.VMEM((GROUP_TILES_B, TQ_B, WIN_B), F32),
            pltpu.VMEM((GROUP_TILES_B, TQ_B, HEAD_DIM), F32),
            per_branch, per_branch,
        ],
        compiler_params=_params(("parallel", "parallel"), 56),
        name="mixer_b",
    )(qb, kb, vb, bias)


def _outproj_kernel(oa_ref, ob_ref, x_ref, ga_ref, gb_ref, w_ref, gffn_ref, x1_ref, h2_ref):
    oa, ob = oa_ref[...], ob_ref[...]
    na = (oa * _rms_scale(oa) * ga_ref[...]).astype(BF16)
    nb = (ob * _rms_scale(ob) * gb_ref[...]).astype(BF16)
    x1 = (x_ref[...]
          + jnp.dot(na, w_ref[:WIDTH_A, :], preferred_element_type=F32)
          + jnp.dot(nb, w_ref[WIDTH_A:, :], preferred_element_type=F32))
    x1_ref[...] = x1
    h2_ref[...] = (x1 * _rms_scale(x1) * gffn_ref[...]).astype(BF16)


def _out_projection(oa, ob, x2, g_a, g_b, w_out, g_ffn):
    m = x2.shape[0]
    tm = TM_OUT
    row = lambda i: (i, 0)
    fixed = lambda i: (0, 0)
    return pl.pallas_call(
        _outproj_kernel,
        grid=(m // tm,),
        in_specs=[
            pl.BlockSpec((tm, WIDTH_A), row),
            pl.BlockSpec((tm, WIDTH_B), row),
            pl.BlockSpec((tm, D_MODEL), row),
            pl.BlockSpec((1, WIDTH_A), fixed),
            pl.BlockSpec((1, WIDTH_B), fixed),
            pl.BlockSpec((WIDTH_A + WIDTH_B, D_MODEL), fixed, pipeline_mode=pl.Buffered(1)),
            pl.BlockSpec((1, D_MODEL), fixed),
        ],
        out_specs=[pl.BlockSpec((tm, D_MODEL), row), pl.BlockSpec((tm, D_MODEL), row)],
        out_shape=[jax.ShapeDtypeStruct((m, D_MODEL), F32), jax.ShapeDtypeStruct((m, D_MODEL), BF16)],
        compiler_params=_params(("parallel",), 40),
        name="out_projection",
    )(oa, ob, x2, g_a, g_b, w_out, g_ffn)


def _ffn_kernel(nf, h_ref, x1_ref, wg_ref, wu_ref, wd_ref, gfin_ref, o_ref, acc_ref, act_ref):
    j = pl.program_id(0)

    @pl.when(j == 0)
    def _():
        act_ref[...] = jnp.zeros(act_ref.shape, BF16)

    f_done = jnp.maximum(j - 1, 0) % nf
    part = jnp.dot(act_ref[...], wd_ref[...], preferred_element_type=F32)
    acc_ref[...] = jnp.where(f_done == 0, part, acc_ref[...] + part)

    h = h_ref[...]
    gate = jnp.dot(h, wg_ref[...], preferred_element_type=F32)
    up = jnp.dot(h, wu_ref[...], preferred_element_type=F32)
    act_ref[...] = (gate * jax.nn.sigmoid(gate) * up).astype(BF16)

    @pl.when(jnp.logical_and(j > 0, f_done == nf - 1))
    def _():
        y = x1_ref[...] + acc_ref[...]
        o_ref[...] = y * _rms_scale(y) * gfin_ref[...]


def _ffn(h2, x1, w_gate_up, w_down, g_final):
    m = h2.shape[0]
    tm, tf = TM_FFN, TF_FFN
    nf = D_FF // tf
    n = (m // tm) * nf

    def started(j):
        t = jnp.minimum(j, n - 1)
        return t // nf, t % nf

    def finished(j):
        t = jnp.maximum(j - 1, 0)
        return t // nf, t % nf

    return pl.pallas_call(
        functools.partial(_ffn_kernel, nf),
        grid=(n + 1,),
        in_specs=[
            pl.BlockSpec((tm, D_MODEL), lambda j: (started(j)[0], 0)),
            pl.BlockSpec((tm, D_MODEL), lambda j: (finished(j)[0], 0)),
            pl.BlockSpec((D_MODEL, tf), lambda j: (0, started(j)[1])),
            pl.BlockSpec((D_MODEL, tf), lambda j: (0, started(j)[1] + nf)),
            pl.BlockSpec((tf, D_MODEL), lambda j: (finished(j)[1], 0)),
            pl.BlockSpec((1, D_MODEL), lambda j: (0, 0)),
        ],
        out_specs=pl.BlockSpec((tm, D_MODEL), lambda j: (finished(j)[0], 0)),
        out_shape=jax.ShapeDtypeStruct((m, D_MODEL), F32),
        scratch_shapes=[pltpu.VMEM((tm, D_MODEL), F32), pltpu.VMEM((tm, tf), BF16)],
        compiler_params=_params(("arbitrary",), 48),
        name="ffn",
    )(h2, x1, w_gate_up, w_gate_up, w_down, g_final)


def kernel(x, g_mix, w_in, g_q_a, g_k_a, g_out_a, g_out_b, w_out, g_ffn, w_gate_up, w_down, g_final):
    batch, seq, d = x.shape
    assert d == D_MODEL and seq % GRID_W == 0
    assert seq % TM_PROJ == 0 and seq == P4 * GROUP_TILES_B * TQ_B and seq == max(DILATIONS) * WIN_B
    depth = w_in.shape[0]
    tables = _rope_tables(seq)
    x2 = x.reshape(batch * seq, d)
    for layer in range(depth):
        qa, ka, va, qb, kb, vb = _in_projection(
            x2, g_mix[layer][None], w_in[layer].astype(BF16),
            g_q_a[layer][None], g_k_a[layer][None], tables, seq)
        oa = _mixer_a(qa, ka, va, batch, seq)
        ob = _mixer_b(qb, kb, vb, batch, seq)
        x1, h2 = _out_projection(oa, ob, x2, g_out_a[layer][None], g_out_b[layer][None],
                                 w_out[layer].astype(BF16), g_ffn[layer][None])
        assert depth == 1
        x2 = _ffn(h2, x1, w_gate_up[layer].astype(BF16), w_down[layer].astype(BF16), g_final[None])
    return x2.reshape(batch, seq, d)
```

```python
import functools
import math

import jax
import jax.numpy as jnp
from jax import lax
from jax.experimental import pallas as pl
from jax.experimental.pallas import tpu as pltpu

F32 = jnp.float32
BF16 = jnp.bfloat16

D_MODEL = 2048
HEAD_DIM = 128
N_HEADS_A = 8
N_KV_A = 2
GROUP_A = N_HEADS_A // N_KV_A
N_HEADS_B = 8
WIDTH_A = N_HEADS_A * HEAD_DIM
WIDTH_B = N_HEADS_B * HEAD_DIM
KV_A = N_KV_A * HEAD_DIM
PROJ_OUT = WIDTH_A + 2 * KV_A + 3 * WIDTH_B
GRID_W = 64
AXIAL_THETA = 10000.0
ROPE_THETA = 500000.0
PARTIAL_ROT = HEAD_DIM // 4
DILATIONS = (1, 4, 16)
BAND = 64
D_FF = 5632
EPS = 1e-6
Q_SCALE = HEAD_DIM ** -0.5 * math.log2(math.e)

MIB = 1024 * 1024

TM_PROJ = 512
TQ_A = 256
TK_A = 512
TQ_B = 128
WIN_B = TQ_B + 2 * BAND
P4 = 4
GROUP_TILES_B = 8
TM_OUT = 512
TM_FFN = 1024
FFN_SUB_ROWS = 512
TF_FFN = 512


def _params(semantics, vmem_mib):
    return pltpu.CompilerParams(dimension_semantics=semantics, vmem_limit_bytes=vmem_mib * MIB)


def _rms_scale(y):
    return lax.rsqrt(jnp.mean(y * y, axis=-1, keepdims=True) + EPS)


def _rope_tables(seq):
    def angles(pos, dim, theta):
        inv = theta ** (-(jnp.arange(0, dim, 2, dtype=F32) / dim))
        return pos[:, None] * inv[None, :]

    half = HEAD_DIM // 2
    rows = seq // GRID_W
    row_pos = jnp.repeat(jnp.arange(rows, dtype=F32), GRID_W)
    col_pos = jnp.tile(jnp.arange(GRID_W, dtype=F32), rows)
    ar, ac = angles(row_pos, half, AXIAL_THETA), angles(col_pos, half, AXIAL_THETA)
    z = jnp.zeros_like(ar)
    cos_a = jnp.concatenate([jnp.cos(ar), jnp.cos(ar), jnp.cos(ac), jnp.cos(ac)], axis=-1)
    up_a = jnp.concatenate([-jnp.sin(ar), z, -jnp.sin(ac), z], axis=-1)
    dn_a = jnp.concatenate([z, jnp.sin(ar), z, jnp.sin(ac)], axis=-1)

    ab = angles(jnp.arange(seq, dtype=F32), PARTIAL_ROT, ROPE_THETA)
    hw = PARTIAL_ROT // 2
    one_rest = jnp.ones((seq, HEAD_DIM - PARTIAL_ROT), F32)
    cos_b = jnp.concatenate([jnp.cos(ab), jnp.cos(ab), one_rest], axis=-1)
    up_b = jnp.concatenate([-jnp.sin(ab), jnp.zeros((seq, HEAD_DIM - hw), F32)], axis=-1)
    dn_b = jnp.concatenate([jnp.zeros((seq, hw), F32), jnp.sin(ab),
                            jnp.zeros((seq, HEAD_DIM - PARTIAL_ROT), F32)], axis=-1)
    return cos_a, up_a, dn_a, cos_b, up_b, dn_b


def _rotate(t, cos, s_up, s_dn, shift):
    return (t * cos + pltpu.roll(t, HEAD_DIM - shift, 1) * s_up
            + pltpu.roll(t, shift, 1) * s_dn)


def _inproj_kernel(x_ref, gmix_ref, w_ref, gq_ref, gk_ref,
                   ca_ref, ua_ref, da_ref, cb_ref, ub_ref, db_ref,
                   qa_ref, ka_ref, va_ref, qb_ref, kb_ref, vb_ref):
    x = x_ref[...]
    h = (x * _rms_scale(x) * gmix_ref[...]).astype(BF16)
    ca, ua, da = ca_ref[...], ua_ref[...], da_ref[...]
    cb, ub, db = cb_ref[...], ub_ref[...], db_ref[...]
    gq, gk = gq_ref[...], gk_ref[...]
    ones = jnp.ones((x.shape[0], HEAD_DIM), BF16)

    pair = {}
    head_sum = (lax.broadcasted_iota(jnp.int32, (2 * HEAD_DIM, 2 * HEAD_DIM), 0) // HEAD_DIM
                == lax.broadcasted_iota(jnp.int32, (2 * HEAD_DIM, 2 * HEAD_DIM), 1) // HEAD_DIM
                ).astype(BF16)

    def head(col):
        base = col - col % (2 * HEAD_DIM)
        if base not in pair:
            pair.clear()
            pair[base] = jnp.dot(h, w_ref[:, base:base + 2 * HEAD_DIM], preferred_element_type=F32)
        return pair[base][:, col - base:col - base + HEAD_DIM]

    qa0, ka0, va0 = 0, WIDTH_A, WIDTH_A + KV_A
    qb0 = WIDTH_A + 2 * KV_A
    kb0, vb0 = qb0 + WIDTH_B, qb0 + 2 * WIDTH_B
    for i in range(N_HEADS_B):
        qb_ref[i] = _rotate(head(qb0 + i * HEAD_DIM), cb, ub, db, PARTIAL_ROT // 2) * Q_SCALE
    for i in range(N_HEADS_B):
        kb_ref[i] = _rotate(head(kb0 + i * HEAD_DIM), cb, ub, db, PARTIAL_ROT // 2)

    ya = [head(c) for c in range(qa0, va0, HEAD_DIM)]
    z = jnp.concatenate([jnp.concatenate([ya[i] * ya[i], ya[i + 1] * ya[i + 1]], axis=1)
                         for i in range(0, len(ya), 2)], axis=0)
    z_hi = z.astype(BF16)
    z_lo = (z - z_hi.astype(F32)).astype(BF16)
    ssq = (jnp.dot(z_hi, head_sum, preferred_element_type=F32)
           + jnp.dot(z_lo, head_sum, preferred_element_type=F32))
    r = lax.rsqrt(ssq * (1.0 / HEAD_DIM) + EPS)
    tm = x.shape[0]
    for i, y in enumerate(ya):
        g, scale = (gq, Q_SCALE) if i < N_HEADS_A else (gk, 1.0)
        ri = r[(i // 2) * tm:(i // 2 + 1) * tm, (i % 2) * HEAD_DIM:(i % 2 + 1) * HEAD_DIM]
        out = (_rotate(y * g, ca, ua, da, HEAD_DIM // 4) * (ri * scale)).astype(BF16)
        if i < N_HEADS_A:
            qa_ref[i] = out
        else:
            ka_ref[i - N_HEADS_A] = out
    for i in range(N_KV_A):
        va_ref[i, :, :HEAD_DIM] = head(va0 + i * HEAD_DIM).astype(BF16)
        va_ref[i, :, HEAD_DIM:] = ones
    for i in range(N_HEADS_B):
        vb_ref[i] = head(vb0 + i * HEAD_DIM)


def _in_projection(x2, g_mix, w_in, g_q, g_k, tables, seq):
    m = x2.shape[0]
    tm = TM_PROJ
    steps_per_seq = seq // tm
    row = lambda i: (i, 0)
    fixed = lambda i: (0, 0)
    tab = pl.BlockSpec((tm, HEAD_DIM), lambda i: (i % steps_per_seq, 0))
    heads = lambda n: pl.BlockSpec((n, tm, HEAD_DIM), lambda i: (0, i, 0))
    return pl.pallas_call(
        _inproj_kernel,
        grid=(m // tm,),
        in_specs=[
            pl.BlockSpec((tm, D_MODEL), row),
            pl.BlockSpec((1, D_MODEL), fixed),
            pl.BlockSpec((D_MODEL, PROJ_OUT), fixed, pipeline_mode=pl.Buffered(1)),
            pl.BlockSpec((1, HEAD_DIM), fixed),
            pl.BlockSpec((1, HEAD_DIM), fixed),
            tab, tab, tab, tab, tab, tab,
        ],
        out_specs=[
            heads(N_HEADS_A),
            heads(N_KV_A),
            pl.BlockSpec((N_KV_A, tm, 2 * HEAD_DIM), lambda i: (0, i, 0)),
            heads(N_HEADS_B), heads(N_HEADS_B), heads(N_HEADS_B),
        ],
        out_shape=[
            jax.ShapeDtypeStruct((N_HEADS_A, m, HEAD_DIM), BF16),
            jax.ShapeDtypeStruct((N_KV_A, m, HEAD_DIM), BF16),
            jax.ShapeDtypeStruct((N_KV_A, m, 2 * HEAD_DIM), BF16),
            jax.ShapeDtypeStruct((N_HEADS_B, m, HEAD_DIM), F32),
            jax.ShapeDtypeStruct((N_HEADS_B, m, HEAD_DIM), F32),
            jax.ShapeDtypeStruct((N_HEADS_B, m, HEAD_DIM), F32),
        ],
        compiler_params=_params(("parallel",), 56),
        name="in_projection",
    )(x2, g_mix, w_in, g_q, g_k, *tables)


def _mixer_a_kernel(q_ref, k_ref, v_ref, o_ref, s_ref, m_ref):
    tq = q_ref.shape[1]
    seq = k_ref.shape[1]

    @pl.when(pl.program_id(0) == 0)
    def _():
        s_ref[...] = jnp.zeros(s_ref.shape, F32)
        m_ref[...] = jnp.zeros(m_ref.shape, F32)

    q = jnp.concatenate([q_ref[h] for h in range(GROUP_A)], axis=0)
    m_done = m_ref[:, :1]
    acc = None
    m_run = None
    for c in range(0, seq, TK_A):
        p = jnp.exp2(s_ref[:, c:c + TK_A] - m_done).astype(BF16)
        pv = jnp.dot(p, v_ref[0, c:c + TK_A, :], preferred_element_type=F32)
        acc = pv if acc is None else acc + pv
        s = lax.dot_general(q, k_ref[0, c:c + TK_A, :], (((1,), (1,)), ((), ())),
                            preferred_element_type=F32)
        s_ref[:, c:c + TK_A] = s
        for l in range(0, TK_A, HEAD_DIM):
            part = s[:, l:l + HEAD_DIM]
            m_run = part if m_run is None else jnp.maximum(m_run, part)
    m_ref[...] = jnp.broadcast_to(jnp.max(m_run, axis=-1, keepdims=True), m_ref.shape)
    o = acc[:, :HEAD_DIM] / acc[:, HEAD_DIM:]
    for h in range(GROUP_A):
        o_ref[:, h * HEAD_DIM:(h + 1) * HEAD_DIM] = o[h * tq:(h + 1) * tq].astype(o_ref.dtype)


def _mixer_a(qa, ka, va, batch, seq):
    m = qa.shape[1]
    tq = TQ_A
    tiles = seq // tq
    gw = GROUP_A * HEAD_DIM
    n = batch * N_KV_A * tiles

    def coords(t):
        bg, i = t // tiles, t % tiles
        b, g = bg // N_KV_A, bg % N_KV_A
        return b * tiles + i, g, b

    def started(j):
        return coords(jnp.minimum(j, n - 1))

    def finished(j):
        return coords(jnp.maximum(j - 1, 0))

    return pl.pallas_call(
        _mixer_a_kernel,
        grid=(n + 1,),
        in_specs=[
            pl.BlockSpec((GROUP_A, tq, HEAD_DIM), lambda j: (started(j)[1], started(j)[0], 0)),
            pl.BlockSpec((1, seq, HEAD_DIM), lambda j: (started(j)[1], started(j)[2], 0)),
            pl.BlockSpec((1, seq, 2 * HEAD_DIM), lambda j: (finished(j)[1], finished(j)[2], 0)),
        ],
        out_specs=pl.BlockSpec((tq, gw), lambda j: finished(j)[:2]),
        out_shape=jax.ShapeDtypeStruct((m, WIDTH_A), BF16),
        scratch_shapes=[pltpu.VMEM((GROUP_A * tq, seq), F32),
                        pltpu.VMEM((GROUP_A * tq, HEAD_DIM), F32)],
        compiler_params=_params(("arbitrary",), 48),
        name="mixer_a",
    )(qa, ka, va)


def _mixer_b_bias(seq):
    i = jnp.arange(TQ_B)[:, None]
    j = jnp.arange(WIN_B)[None, :]
    q4, k4 = TQ_B // P4, WIN_B // P4
    kinds = [P4 * (off + i % q4 - j % k4) + i // q4 - j // k4 for off in (0, BAND // P4, 2 * BAND // P4)]
    kinds += [off + i - j for off in (0, BAND, 2 * BAND)]
    kinds += [off + i - j for off in (0, TQ_B)]
    return jnp.stack([jnp.where(jnp.abs(dist) <= BAND, 0.0, -jnp.inf).astype(F32) for dist in kinds])


def _mixer_b_kernel(q_ref, k_ref, v_ref, bias_ref, o_ref,
                    qf_ref, kf_ref, vf_ref, qb_ref, kb_ref, vb_ref, s_ref, m_ref, part_ref, lse_ref):
    seq = q_ref.shape[0]
    chunk = seq // P4
    g_tiles = s_ref.shape[0]
    n_groups = seq // TQ_B // g_tiles
    assert chunk == g_tiles * TQ_B and chunk == P4 * WIN_B

    vb_ref[:, HEAD_DIM:] = jnp.ones((seq, HEAD_DIM), BF16)
    for r in range(P4):
        dst = slice(r * chunk, (r + 1) * chunk)
        for src, f_ref, b_ref in ((q_ref, qf_ref, qb_ref), (k_ref, kf_ref, kb_ref), (v_ref, vf_ref, vb_ref)):
            x = src[pl.ds(r, chunk, stride=P4), :]
            f_ref[dst, :] = x
            b_ref[dst, :HEAD_DIM] = x.astype(BF16)

    def pieces(ref, start, rows):
        return jnp.concatenate([ref[pl.ds(r * chunk + start, rows), :] for r in range(P4)], axis=0)

    def tile_rows(branch, g, t):
        if branch == 0:
            q0 = pl.multiple_of(g * (g_tiles * TQ_B // P4) + t * (TQ_B // P4), TQ_B // P4)
            k0 = pl.multiple_of(jnp.clip(q0 - BAND // P4, 0, chunk - WIN_B // P4), BAND // P4)
            kind = (q0 - k0) // (BAND // P4)

            def put(ref, val):
                for r in range(P4):
                    ref[0, pl.ds(r * chunk + q0, TQ_B // P4), :] = val[r * (TQ_B // P4):(r + 1) * (TQ_B // P4)]
            return (lambda: pieces(qb_ref, q0, TQ_B // P4), lambda: pieces(kb_ref, k0, WIN_B // P4),
                    lambda: pieces(vb_ref, k0, WIN_B // P4), kind, put)
        base = pl.multiple_of(g * chunk, chunk)
        if branch == 1:
            q0 = t * TQ_B
            k0 = min(max(q0 - BAND, 0), chunk - WIN_B)
            kind = 3 + (q0 - k0) // BAND

            def put(ref, val):
                ref[1, pl.ds(base + q0, TQ_B), :] = val
            return (lambda: qb_ref[pl.ds(base + q0, TQ_B), :], lambda: kb_ref[pl.ds(base + k0, WIN_B), :],
                    lambda: vb_ref[pl.ds(base + k0, WIN_B), :], kind, put)
        a, u = t // 2, t % 2
        q_rows = pl.ds(base + a + P4 * TQ_B * u, TQ_B, stride=P4)
        k_rows = pl.ds(base + a, WIN_B, stride=P4)

        def put(ref, val):
            ref[2, q_rows, :] = val
        return (lambda: qf_ref[q_rows, :].astype(BF16), lambda: kf_ref[k_rows, :].astype(BF16),
                lambda: jnp.concatenate([vf_ref[k_rows, :].astype(BF16), jnp.ones((WIN_B, HEAD_DIM), BF16)], axis=1),
                6 + u, put)

    def start(branch, g, t):
        q, k, _, kind, _ = tile_rows(branch, g, t)
        s = lax.dot_general(q(), k(), (((1,), (1,)), ((), ())), preferred_element_type=F32) + bias_ref[kind]
        s_ref[t] = s
        m_ref[t] = jnp.broadcast_to(jnp.max(s, axis=-1, keepdims=True), (TQ_B, HEAD_DIM))

    def finish(branch, g, t):
        _, _, v, _, put = tile_rows(branch, g, t)
        m = m_ref[t]
        p = jnp.concatenate([jnp.exp2(s_ref[t, :, c:c + HEAD_DIM] - m) for c in range(0, WIN_B, HEAD_DIM)],
                            axis=1).astype(BF16)
        pv = jnp.dot(p, v(), preferred_element_type=F32)
        den = pv[:, HEAD_DIM:]
        put(part_ref, pv[:, :HEAD_DIM] / den)
        put(lse_ref, m + jnp.log2(den))

    for t in range(g_tiles):
        start(0, 0, t)
    for branch in range(len(DILATIONS)):
        def body(g, carry, branch=branch):
            for t in range(g_tiles):
                finish(branch, g - 1, t)
                start(branch, g, t)
            return carry

        lax.fori_loop(1, n_groups, body, 0)
        for t in range(g_tiles):
            finish(branch, n_groups - 1, t)
            if branch + 1 < len(DILATIONS):
                start(branch + 1, 0, t)

    for r in range(P4):
        rows = slice(r * chunk, (r + 1) * chunk)
        lse = [lse_ref[b, rows, :] for b in range(len(DILATIONS))]
        top = jnp.maximum(jnp.maximum(lse[0], lse[1]), lse[2])
        w = [jnp.exp2(l - top) for l in lse]
        num = w[0] * part_ref[0, rows, :] + w[1] * part_ref[1, rows, :] + w[2] * part_ref[2, rows, :]
        qf_ref[pl.ds(r, chunk, stride=P4), :] = num / (w[0] + w[1] + w[2])
    o_ref[...] = qf_ref[...].astype(o_ref.dtype)


def _mixer_b(qb, kb, vb, batch, seq):
    m = qb.shape[1]
    spec = pl.BlockSpec((None, seq, HEAD_DIM), lambda b, h: (h, b, 0))
    bias = _mixer_b_bias(seq)
    f32_rows = pltpu.VMEM((seq, HEAD_DIM), F32)
    bf16_rows = pltpu.VMEM((seq, HEAD_DIM), BF16)
    per_branch = pltpu.VMEM((len(DILATIONS), seq, HEAD_DIM), F32)
    return pl.pallas_call(
        _mixer_b_kernel,
        grid=(batch, N_HEADS_B),
        in_specs=[spec, spec, spec, pl.BlockSpec(bias.shape, lambda b, h: (0, 0, 0))],
        out_specs=pl.BlockSpec((seq, HEAD_DIM), lambda b, h: (b, h)),
        out_shape=jax.ShapeDtypeStruct((m, WIDTH_B), BF16),
        scratch_shapes=[
            f32_rows, f32_rows, f32_rows,
            bf16_rows, bf16_rows, pltpu.VMEM((seq, 2 * HEAD_DIM), BF16),
            pltpu.VMEM((GROUP_TILES_B, TQ_B, WIN_B), F32),
            pltpu.VMEM((GROUP_TILES_B, TQ_B, HEAD_DIM), F32),
            per_branch, per_branch,
        ],
        compiler_params=_params(("parallel", "parallel"), 56),
        name="mixer_b",
    )(qb, kb, vb, bias)


def _outproj_kernel(oa_ref, ob_ref, x_ref, ga_ref, gb_ref, w_ref, gffn_ref, x1_ref, h2_ref):
    oa, ob = oa_ref[...].astype(F32), ob_ref[...].astype(F32)
    na = (oa * _rms_scale(oa) * ga_ref[...]).astype(BF16)
    nb = (ob * _rms_scale(ob) * gb_ref[...]).astype(BF16)
    x1 = (x_ref[...]
          + jnp.dot(na, w_ref[:WIDTH_A, :], preferred_element_type=F32)
          + jnp.dot(nb, w_ref[WIDTH_A:, :], preferred_element_type=F32))
    x1_ref[...] = x1
    h2_ref[...] = (x1 * _rms_scale(x1) * gffn_ref[...]).astype(BF16)


def _out_projection(oa, ob, x2, g_a, g_b, w_out, g_ffn):
    m = x2.shape[0]
    tm = TM_OUT
    row = lambda i: (i, 0)
    fixed = lambda i: (0, 0)
    return pl.pallas_call(
        _outproj_kernel,
        grid=(m // tm,),
        in_specs=[
            pl.BlockSpec((tm, WIDTH_A), row),
            pl.BlockSpec((tm, WIDTH_B), row),
            pl.BlockSpec((tm, D_MODEL), row),
            pl.BlockSpec((1, WIDTH_A), fixed),
            pl.BlockSpec((1, WIDTH_B), fixed),
            pl.BlockSpec((WIDTH_A + WIDTH_B, D_MODEL), fixed, pipeline_mode=pl.Buffered(1)),
            pl.BlockSpec((1, D_MODEL), fixed),
        ],
        out_specs=[pl.BlockSpec((tm, D_MODEL), row), pl.BlockSpec((tm, D_MODEL), row)],
        out_shape=[jax.ShapeDtypeStruct((m, D_MODEL), F32), jax.ShapeDtypeStruct((m, D_MODEL), BF16)],
        compiler_params=_params(("parallel",), 40),
        name="out_projection",
    )(oa, ob, x2, g_a, g_b, w_out, g_ffn)


def _ffn_kernel(nf, h_ref, x1_hbm, wg_ref, wu_ref, wd_ref, gfin_ref, o_ref, acc_ref, act_ref, x1_sem):
    j = pl.program_id(0)
    n = pl.num_programs(0) - 1
    tm = acc_ref.shape[0]
    t_start = jnp.minimum(j, n - 1)
    t_done = jnp.maximum(j - 1, 0)
    f_done = t_done % nf

    def x1_copy(tile):
        return pltpu.make_async_copy(x1_hbm.at[pl.ds(tile * tm, tm), :], acc_ref, x1_sem)

    @pl.when(j == 0)
    def _():
        act_ref[...] = jnp.zeros(act_ref.shape, BF16)
        acc_ref[...] = jnp.zeros(acc_ref.shape, F32)

    @pl.when(jnp.logical_and(j > 0, f_done == 0))
    def _():
        x1_copy(t_done // nf).wait()

    for r in range(0, tm, FFN_SUB_ROWS):
        rows = slice(r, r + FFN_SUB_ROWS)
        acc_ref[rows, :] += jnp.dot(act_ref[rows, :], wd_ref[...], preferred_element_type=F32)
        h = h_ref[rows, :]
        gate = jnp.dot(h, wg_ref[...], preferred_element_type=F32)
        up = jnp.dot(h, wu_ref[...], preferred_element_type=F32)
        act_ref[rows, :] = (gate * jax.nn.sigmoid(gate) * up).astype(BF16)

    @pl.when(jnp.logical_and(j > 0, f_done == nf - 1))
    def _():
        y = acc_ref[...]
        o_ref[...] = y * _rms_scale(y) * gfin_ref[...]

    @pl.when(jnp.logical_and(j < n, t_start % nf == 0))
    def _():
        x1_copy(t_start // nf).start()


def _ffn(h2, x1, w_gate_up, w_down, g_final):
    m = h2.shape[0]
    tm, tf = TM_FFN, TF_FFN
    nf = D_FF // tf
    n = (m // tm) * nf
    w_gu = w_gate_up.reshape(D_MODEL, 2 * nf, tf).transpose(1, 0, 2)

    def started(j):
        t = jnp.minimum(j, n - 1)
        return t // nf, t % nf

    def finished(j):
        t = jnp.maximum(j - 1, 0)
        return t // nf, t % nf

    return pl.pallas_call(
        functools.partial(_ffn_kernel, nf),
        grid=(n + 1,),
        in_specs=[
            pl.BlockSpec((tm, D_MODEL), lambda j: (started(j)[0], 0)),
            pl.BlockSpec(memory_space=pl.ANY),
            pl.BlockSpec((None, D_MODEL, tf), lambda j: (started(j)[1], 0, 0)),
            pl.BlockSpec((None, D_MODEL, tf), lambda j: (started(j)[1] + nf, 0, 0)),
            pl.BlockSpec((tf, D_MODEL), lambda j: (finished(j)[1], 0)),
            pl.BlockSpec((1, D_MODEL), lambda j: (0, 0)),
        ],
        out_specs=pl.BlockSpec((tm, D_MODEL), lambda j: (finished(j)[0], 0)),
        out_shape=jax.ShapeDtypeStruct((m, D_MODEL), F32),
        scratch_shapes=[pltpu.VMEM((tm, D_MODEL), F32), pltpu.VMEM((tm, tf), BF16),
                        pltpu.SemaphoreType.DMA(())],
        compiler_params=_params(("arbitrary",), 56),
        name="ffn",
    )(h2, x1, w_gu, w_gu, w_down, g_final)


def kernel(x, g_mix, w_in, g_q_a, g_k_a, g_out_a, g_out_b, w_out, g_ffn, w_gate_up, w_down, g_final):
    batch, seq, d = x.shape
    assert d == D_MODEL and seq % GRID_W == 0
    assert seq % TM_PROJ == 0 and seq == P4 * GROUP_TILES_B * TQ_B and seq == max(DILATIONS) * WIN_B
    depth = w_in.shape[0]
    tables = _rope_tables(seq)
    x2 = x.reshape(batch * seq, d)
    for layer in range(depth):
        qa, ka, va, qb, kb, vb = _in_projection(
            x2, g_mix[layer][None], w_in[layer].astype(BF16),
            g_q_a[layer][None], g_k_a[layer][None], tables, seq)
        oa = _mixer_a(qa, ka, va, batch, seq)
        ob = _mixer_b(qb, kb, vb, batch, seq)
        x1, h2 = _out_projection(oa, ob, x2, g_out_a[layer][None], g_out_b[layer][None],
                                 w_out[layer].astype(BF16), g_ffn[layer][None])
        assert depth == 1
        x2 = _ffn(h2, x1, w_gate_up[layer].astype(BF16), w_down[layer].astype(BF16), g_final[None])
    return x2.reshape(batch, seq, d)
```

```python
import functools
import math

import jax
import jax.numpy as jnp
import numpy as np
from jax import lax
from jax.experimental import pallas as pl
from jax.experimental.pallas import tpu as pltpu

F32 = jnp.float32
BF16 = jnp.bfloat16

D_MODEL = 2048
HEAD_DIM = 128
N_HEADS_A = 8
N_KV_A = 2
GROUP_A = N_HEADS_A // N_KV_A
N_HEADS_B = 8
WIDTH_A = N_HEADS_A * HEAD_DIM
WIDTH_B = N_HEADS_B * HEAD_DIM
KV_A = N_KV_A * HEAD_DIM
PROJ_OUT = WIDTH_A + 2 * KV_A + 3 * WIDTH_B
GRID_W = 64
AXIAL_THETA = 10000.0
ROPE_THETA = 500000.0
PARTIAL_ROT = HEAD_DIM // 4
DILATIONS = (1, 4, 16)
BAND = 64
D_FF = 5632
EPS = 1e-6
Q_SCALE = HEAD_DIM ** -0.5 * math.log2(math.e)

MIB = 1024 * 1024

TM_PROJ = 512
TQ_A = 256
TK_A = 512
TQ_B = 128
WIN_B = TQ_B + 2 * BAND
P4 = 4
GROUP_TILES_B = 8
TM_OUT = 512
TM_FFN = 1024
FFN_SUB_ROWS = 512
TF_FFN = 512


def _params(semantics, vmem_mib):
    return pltpu.CompilerParams(dimension_semantics=semantics, vmem_limit_bytes=vmem_mib * MIB)


def _rms_scale(y):
    return lax.rsqrt(jnp.mean(y * y, axis=-1, keepdims=True) + EPS)


def _rope_tables(seq):
    def angles(pos, dim, theta):
        inv = theta ** (-(np.arange(0, dim, 2, dtype=np.float64) / dim))
        return pos[:, None] * inv[None, :]

    half = HEAD_DIM // 2
    rows = seq // GRID_W
    row_pos = np.repeat(np.arange(rows, dtype=np.float64), GRID_W)
    col_pos = np.tile(np.arange(GRID_W, dtype=np.float64), rows)
    ar, ac = angles(row_pos, half, AXIAL_THETA), angles(col_pos, half, AXIAL_THETA)
    z = np.zeros_like(ar)
    cos_a = np.concatenate([np.cos(ar), np.cos(ar), np.cos(ac), np.cos(ac)], axis=-1)
    up_a = np.concatenate([-np.sin(ar), z, -np.sin(ac), z], axis=-1)
    dn_a = np.concatenate([z, np.sin(ar), z, np.sin(ac)], axis=-1)

    ab = angles(np.arange(seq, dtype=np.float64), PARTIAL_ROT, ROPE_THETA)
    hw = PARTIAL_ROT // 2
    one_rest = np.ones((seq, HEAD_DIM - PARTIAL_ROT))
    cos_b = np.concatenate([np.cos(ab), np.cos(ab), one_rest], axis=-1)
    up_b = np.concatenate([-np.sin(ab), np.zeros((seq, HEAD_DIM - hw))], axis=-1)
    dn_b = np.concatenate([np.zeros((seq, hw)), np.sin(ab), np.zeros((seq, HEAD_DIM - PARTIAL_ROT))], axis=-1)
    return tuple(jnp.asarray(t, dtype=F32) for t in (cos_a, up_a, dn_a, cos_b, up_b, dn_b))


def _rotate(t, cos, s_up, s_dn, shift):
    return (t * cos + pltpu.roll(t, HEAD_DIM - shift, 1) * s_up
            + pltpu.roll(t, shift, 1) * s_dn)


def _inproj_kernel(x_ref, gmix_ref, w_ref, gq_ref, gk_ref,
                   ca_ref, ua_ref, da_ref, cb_ref, ub_ref, db_ref,
                   qa_ref, ka_ref, va_ref, qb_ref, kb_ref, vb_ref):
    x = x_ref[...]
    h = (x * _rms_scale(x) * gmix_ref[...]).astype(BF16)
    ca, ua, da = ca_ref[...], ua_ref[...], da_ref[...]
    cb, ub, db = cb_ref[...], ub_ref[...], db_ref[...]
    gq, gk = gq_ref[...], gk_ref[...]
    ones = jnp.ones((x.shape[0], HEAD_DIM), BF16)

    pair = {}
    head_sum = (lax.broadcasted_iota(jnp.int32, (2 * HEAD_DIM, 2 * HEAD_DIM), 0) // HEAD_DIM
                == lax.broadcasted_iota(jnp.int32, (2 * HEAD_DIM, 2 * HEAD_DIM), 1) // HEAD_DIM
                ).astype(BF16)

    def head(col):
        base = col - col % (2 * HEAD_DIM)
        if base not in pair:
            pair.clear()
            pair[base] = jnp.dot(h, w_ref[:, base:base + 2 * HEAD_DIM], preferred_element_type=F32)
        return pair[base][:, col - base:col - base + HEAD_DIM]

    qa0, ka0, va0 = 0, WIDTH_A, WIDTH_A + KV_A
    qb0 = WIDTH_A + 2 * KV_A
    kb0, vb0 = qb0 + WIDTH_B, qb0 + 2 * WIDTH_B
    for i in range(N_HEADS_B):
        qb_ref[i] = _rotate(head(qb0 + i * HEAD_DIM), cb, ub, db, PARTIAL_ROT // 2) * Q_SCALE
    for i in range(N_HEADS_B):
        kb_ref[i] = _rotate(head(kb0 + i * HEAD_DIM), cb, ub, db, PARTIAL_ROT // 2)

    ya = [head(c) for c in range(qa0, va0, HEAD_DIM)]
    z = jnp.concatenate([jnp.concatenate([ya[i] * ya[i], ya[i + 1] * ya[i + 1]], axis=1)
                         for i in range(0, len(ya), 2)], axis=0)
    z_hi = z.astype(BF16)
    z_lo = (z - z_hi.astype(F32)).astype(BF16)
    ssq = (jnp.dot(z_hi, head_sum, preferred_element_type=F32)
           + jnp.dot(z_lo, head_sum, preferred_element_type=F32))
    r = lax.rsqrt(ssq * (1.0 / HEAD_DIM) + EPS)
    tm = x.shape[0]
    for i, y in enumerate(ya):
        g, scale = (gq, Q_SCALE) if i < N_HEADS_A else (gk, 1.0)
        ri = r[(i // 2) * tm:(i // 2 + 1) * tm, (i % 2) * HEAD_DIM:(i % 2 + 1) * HEAD_DIM]
        out = (_rotate(y * g, ca, ua, da, HEAD_DIM // 4) * (ri * scale)).astype(BF16)
        if i < N_HEADS_A:
            qa_ref[i] = out
        else:
            ka_ref[i - N_HEADS_A] = out
    for i in range(N_KV_A):
        va_ref[i, :, :HEAD_DIM] = head(va0 + i * HEAD_DIM).astype(BF16)
        va_ref[i, :, HEAD_DIM:] = ones
    for i in range(N_HEADS_B):
        vb_ref[i] = head(vb0 + i * HEAD_DIM)


def _in_projection(x2, g_mix, w_in, g_q, g_k, tables, seq):
    m = x2.shape[0]
    tm = TM_PROJ
    steps_per_seq = seq // tm
    row = lambda i: (i, 0)
    fixed = lambda i: (0, 0)
    tab = pl.BlockSpec((tm, HEAD_DIM), lambda i: (i % steps_per_seq, 0))
    heads = lambda n: pl.BlockSpec((n, tm, HEAD_DIM), lambda i: (0, i, 0))
    return pl.pallas_call(
        _inproj_kernel,
        grid=(m // tm,),
        in_specs=[
            pl.BlockSpec((tm, D_MODEL), row),
            pl.BlockSpec((1, D_MODEL), fixed),
            pl.BlockSpec((D_MODEL, PROJ_OUT), fixed, pipeline_mode=pl.Buffered(1)),
            pl.BlockSpec((1, HEAD_DIM), fixed),
            pl.BlockSpec((1, HEAD_DIM), fixed),
            tab, tab, tab, tab, tab, tab,
        ],
        out_specs=[
            heads(N_HEADS_A),
            heads(N_KV_A),
            pl.BlockSpec((N_KV_A, tm, 2 * HEAD_DIM), lambda i: (0, i, 0)),
            heads(N_HEADS_B), heads(N_HEADS_B), heads(N_HEADS_B),
        ],
        out_shape=[
            jax.ShapeDtypeStruct((N_HEADS_A, m, HEAD_DIM), BF16),
            jax.ShapeDtypeStruct((N_KV_A, m, HEAD_DIM), BF16),
            jax.ShapeDtypeStruct((N_KV_A, m, 2 * HEAD_DIM), BF16),
            jax.ShapeDtypeStruct((N_HEADS_B, m, HEAD_DIM), F32),
            jax.ShapeDtypeStruct((N_HEADS_B, m, HEAD_DIM), F32),
            jax.ShapeDtypeStruct((N_HEADS_B, m, HEAD_DIM), F32),
        ],
        compiler_params=_params(("parallel",), 56),
        name="in_projection",
    )(x2, g_mix, w_in, g_q, g_k, *tables)


def _mixer_a_kernel(q_ref, k_ref, v_ref, o_ref, s_ref, m_ref):
    tq = q_ref.shape[1]
    seq = k_ref.shape[1]

    @pl.when(pl.program_id(0) == 0)
    def _():
        s_ref[...] = jnp.zeros(s_ref.shape, F32)
        m_ref[...] = jnp.zeros(m_ref.shape, F32)

    q = jnp.concatenate([q_ref[h] for h in range(GROUP_A)], axis=0)
    m_done = m_ref[:, :1]
    acc = None
    m_run = None
    for c in range(0, seq, TK_A):
        p = jnp.exp2(s_ref[:, c:c + TK_A] - m_done).astype(BF16)
        pv = jnp.dot(p, v_ref[0, c:c + TK_A, :], preferred_element_type=F32)
        acc = pv if acc is None else acc + pv
        s = lax.dot_general(q, k_ref[0, c:c + TK_A, :], (((1,), (1,)), ((), ())),
                            preferred_element_type=F32)
        s_ref[:, c:c + TK_A] = s
        for l in range(0, TK_A, HEAD_DIM):
            part = s[:, l:l + HEAD_DIM]
            m_run = part if m_run is None else jnp.maximum(m_run, part)
    m_ref[...] = jnp.broadcast_to(jnp.max(m_run, axis=-1, keepdims=True), m_ref.shape)
    o = acc[:, :HEAD_DIM] / acc[:, HEAD_DIM:]
    for h in range(GROUP_A):
        o_ref[:, h * HEAD_DIM:(h + 1) * HEAD_DIM] = o[h * tq:(h + 1) * tq].astype(o_ref.dtype)


def _mixer_a(qa, ka, va, batch, seq):
    m = qa.shape[1]
    tq = TQ_A
    tiles = seq // tq
    gw = GROUP_A * HEAD_DIM
    n = batch * N_KV_A * tiles

    def coords(t):
        bg, i = t // tiles, t % tiles
        b, g = bg // N_KV_A, bg % N_KV_A
        return b * tiles + i, g, b

    def started(j):
        return coords(jnp.minimum(j, n - 1))

    def finished(j):
        return coords(jnp.maximum(j - 1, 0))

    return pl.pallas_call(
        _mixer_a_kernel,
        grid=(n + 1,),
        in_specs=[
            pl.BlockSpec((GROUP_A, tq, HEAD_DIM), lambda j: (started(j)[1], started(j)[0], 0)),
            pl.BlockSpec((1, seq, HEAD_DIM), lambda j: (started(j)[1], started(j)[2], 0)),
            pl.BlockSpec((1, seq, 2 * HEAD_DIM), lambda j: (finished(j)[1], finished(j)[2], 0)),
        ],
        out_specs=pl.BlockSpec((tq, gw), lambda j: finished(j)[:2]),
        out_shape=jax.ShapeDtypeStruct((m, WIDTH_A), BF16),
        scratch_shapes=[pltpu.VMEM((GROUP_A * tq, seq), F32),
                        pltpu.VMEM((GROUP_A * tq, HEAD_DIM), F32)],
        compiler_params=_params(("arbitrary",), 48),
        name="mixer_a",
    )(qa, ka, va)


def _mixer_b_bias(seq):
    i = np.arange(TQ_B)[:, None]
    j = np.arange(WIN_B)[None, :]
    q4, k4 = TQ_B // P4, WIN_B // P4
    kinds = [P4 * (off + i % q4 - j % k4) + i // q4 - j // k4 for off in (0, BAND // P4, 2 * BAND // P4)]
    kinds += [off + i - j for off in (0, BAND, 2 * BAND)]
    kinds += [off + i - j for off in (0, TQ_B)]
    return jnp.asarray(np.stack([np.where(np.abs(dist) <= BAND, 0.0, -np.inf) for dist in kinds]), dtype=F32)


def _mixer_b_kernel(q_ref, k_ref, v_ref, bias_ref, o_ref,
                    qf_ref, kf_ref, vf_ref, qb_ref, kb_ref, vb_ref, s_ref, m_ref, part_ref, lse_ref):
    seq = q_ref.shape[0]
    chunk = seq // P4
    g_tiles = s_ref.shape[0]
    n_groups = seq // TQ_B // g_tiles
    assert chunk == g_tiles * TQ_B and chunk == P4 * WIN_B

    vb_ref[:, HEAD_DIM:] = jnp.ones((seq, HEAD_DIM), BF16)
    for r in range(P4):
        dst = slice(r * chunk, (r + 1) * chunk)
        for src, f_ref, b_ref in ((q_ref, qf_ref, qb_ref), (k_ref, kf_ref, kb_ref), (v_ref, vf_ref, vb_ref)):
            x = src[pl.ds(r, chunk, stride=P4), :]
            f_ref[dst, :] = x
            b_ref[dst, :HEAD_DIM] = x.astype(BF16)

    def pieces(ref, start, rows):
        return jnp.concatenate([ref[pl.ds(r * chunk + start, rows), :] for r in range(P4)], axis=0)

    def tile_rows(branch, g, t):
        if branch == 0:
            q0 = pl.multiple_of(g * (g_tiles * TQ_B // P4) + t * (TQ_B // P4), TQ_B // P4)
            k0 = pl.multiple_of(jnp.clip(q0 - BAND // P4, 0, chunk - WIN_B // P4), BAND // P4)
            kind = (q0 - k0) // (BAND // P4)

            def put(ref, val):
                for r in range(P4):
                    ref[0, pl.ds(r * chunk + q0, TQ_B // P4), :] = val[r * (TQ_B // P4):(r + 1) * (TQ_B // P4)]
            return (lambda: pieces(qb_ref, q0, TQ_B // P4), lambda: pieces(kb_ref, k0, WIN_B // P4),
                    lambda: pieces(vb_ref, k0, WIN_B // P4), kind, put)
        base = pl.multiple_of(g * chunk, chunk)
        if branch == 1:
            q0 = t * TQ_B
            k0 = min(max(q0 - BAND, 0), chunk - WIN_B)
            kind = 3 + (q0 - k0) // BAND

            def put(ref, val):
                ref[1, pl.ds(base + q0, TQ_B), :] = val
            return (lambda: qb_ref[pl.ds(base + q0, TQ_B), :], lambda: kb_ref[pl.ds(base + k0, WIN_B), :],
                    lambda: vb_ref[pl.ds(base + k0, WIN_B), :], kind, put)
        a, u = t // 2, t % 2
        q_rows = pl.ds(base + a + P4 * TQ_B * u, TQ_B, stride=P4)
        k_rows = pl.ds(base + a, WIN_B, stride=P4)

        def put(ref, val):
            ref[2, q_rows, :] = val
        return (lambda: qf_ref[q_rows, :].astype(BF16), lambda: kf_ref[k_rows, :].astype(BF16),
                lambda: jnp.concatenate([vf_ref[k_rows, :].astype(BF16), jnp.ones((WIN_B, HEAD_DIM), BF16)], axis=1),
                6 + u, put)

    def start(branch, g, t):
        q, k, _, kind, _ = tile_rows(branch, g, t)
        s = lax.dot_general(q(), k(), (((1,), (1,)), ((), ())), preferred_element_type=F32) + bias_ref[kind]
        s_ref[t] = s
        m_ref[t] = jnp.broadcast_to(jnp.max(s, axis=-1, keepdims=True), (TQ_B, HEAD_DIM))

    def finish(branch, g, t):
        _, _, v, _, put = tile_rows(branch, g, t)
        m = m_ref[t]
        p = jnp.concatenate([jnp.exp2(s_ref[t, :, c:c + HEAD_DIM] - m) for c in range(0, WIN_B, HEAD_DIM)],
                            axis=1).astype(BF16)
        pv = jnp.dot(p, v(), preferred_element_type=F32)
        den = pv[:, HEAD_DIM:]
        put(part_ref, pv[:, :HEAD_DIM] / den)
        put(lse_ref, m + jnp.log2(den))

    for t in range(g_tiles):
        start(0, 0, t)
    for branch in range(len(DILATIONS)):
        def body(g, carry, branch=branch):
            for t in range(g_tiles):
                finish(branch, g - 1, t)
                start(branch, g, t)
            return carry

        lax.fori_loop(1, n_groups, body, 0)
        for t in range(g_tiles):
            finish(branch, n_groups - 1, t)
            if branch + 1 < len(DILATIONS):
                start(branch + 1, 0, t)

    for r in range(P4):
        rows = slice(r * chunk, (r + 1) * chunk)
        lse = [lse_ref[b, rows, :] for b in range(len(DILATIONS))]
        top = jnp.maximum(jnp.maximum(lse[0], lse[1]), lse[2])
        w = [jnp.exp2(l - top) for l in lse]
        num = w[0] * part_ref[0, rows, :] + w[1] * part_ref[1, rows, :] + w[2] * part_ref[2, rows, :]
        qf_ref[pl.ds(r, chunk, stride=P4), :] = num / (w[0] + w[1] + w[2])
    o_ref[...] = qf_ref[...].astype(o_ref.dtype)


def _mixer_b(qb, kb, vb, batch, seq):
    m = qb.shape[1]
    spec = pl.BlockSpec((None, seq, HEAD_DIM), lambda b, h: (h, b, 0))
    bias = _mixer_b_bias(seq)
    f32_rows = pltpu.VMEM((seq, HEAD_DIM), F32)
    bf16_rows = pltpu.VMEM((seq, HEAD_DIM), BF16)
    per_branch = pltpu.VMEM((len(DILATIONS), seq, HEAD_DIM), F32)
    return pl.pallas_call(
        _mixer_b_kernel,
        grid=(batch, N_HEADS_B),
        in_specs=[spec, spec, spec, pl.BlockSpec(bias.shape, lambda b, h: (0, 0, 0))],
        out_specs=pl.BlockSpec((seq, HEAD_DIM), lambda b, h: (b, h)),
        out_shape=jax.ShapeDtypeStruct((m, WIDTH_B), BF16),
        scratch_shapes=[
            f32_rows, f32_rows, f32_rows,
            bf16_rows, bf16_rows, pltpu.VMEM((seq, 2 * HEAD_DIM), BF16),
            pltpu.VMEM((GROUP_TILES_B, TQ_B, WIN_B), F32),
            pltpu.VMEM((GROUP_TILES_B, TQ_B, HEAD_DIM), F32),
            per_branch, per_branch,
        ],
        compiler_params=_params(("parallel", "parallel"), 56),
        name="mixer_b",
    )(qb, kb, vb, bias)


def _outproj_kernel(oa_ref, ob_ref, x_ref, ga_ref, gb_ref, w_ref, gffn_ref, x1_ref, h2_ref):
    oa, ob = oa_ref[...].astype(F32), ob_ref[...].astype(F32)
    na = (oa * _rms_scale(oa) * ga_ref[...]).astype(BF16)
    nb = (ob * _rms_scale(ob) * gb_ref[...]).astype(BF16)
    x1 = (x_ref[...]
          + jnp.dot(na, w_ref[:WIDTH_A, :], preferred_element_type=F32)
          + jnp.dot(nb, w_ref[WIDTH_A:, :], preferred_element_type=F32))
    x1_ref[...] = x1
    h2_ref[...] = (x1 * _rms_scale(x1) * gffn_ref[...]).astype(BF16)


def _out_projection(oa, ob, x2, g_a, g_b, w_out, g_ffn):
    m = x2.shape[0]
    tm = TM_OUT
    row = lambda i: (i, 0)
    fixed = lambda i: (0, 0)
    return pl.pallas_call(
        _outproj_kernel,
        grid=(m // tm,),
        in_specs=[
            pl.BlockSpec((tm, WIDTH_A), row),
            pl.BlockSpec((tm, WIDTH_B), row),
            pl.BlockSpec((tm, D_MODEL), row),
            pl.BlockSpec((1, WIDTH_A), fixed),
            pl.BlockSpec((1, WIDTH_B), fixed),
            pl.BlockSpec((WIDTH_A + WIDTH_B, D_MODEL), fixed, pipeline_mode=pl.Buffered(1)),
            pl.BlockSpec((1, D_MODEL), fixed),
        ],
        out_specs=[pl.BlockSpec((tm, D_MODEL), row), pl.BlockSpec((tm, D_MODEL), row)],
        out_shape=[jax.ShapeDtypeStruct((m, D_MODEL), F32), jax.ShapeDtypeStruct((m, D_MODEL), BF16)],
        compiler_params=_params(("parallel",), 40),
        name="out_projection",
    )(oa, ob, x2, g_a, g_b, w_out, g_ffn)


def _ffn_kernel(nf, h_ref, x1_hbm, wg_ref, wu_ref, wd_ref, gfin_ref, o_ref, acc_ref, act_ref, x1_sem):
    j = pl.program_id(0)
    tm = acc_ref.shape[0]
    t_done = jnp.maximum(j - 1, 0)
    f_done = t_done % nf
    first, last = jnp.logical_and(j > 0, f_done == 0), jnp.logical_and(j > 0, f_done == nf - 1)

    def x1_copy():
        return pltpu.make_async_copy(x1_hbm.at[pl.ds((t_done // nf) * tm, tm), :], o_ref, x1_sem)

    @pl.when(j == 0)
    def _():
        act_ref[...] = jnp.zeros(act_ref.shape, BF16)
        acc_ref[...] = jnp.zeros(acc_ref.shape, F32)

    @pl.when(first)
    def _():
        x1_copy().start()

    for r in range(0, tm, FFN_SUB_ROWS):
        rows = slice(r, r + FFN_SUB_ROWS)
        part = jnp.dot(act_ref[rows, :], wd_ref[...], preferred_element_type=F32)
        acc_ref[rows, :] = jnp.where(f_done == 0, part, acc_ref[rows, :] + part)
        h = h_ref[rows, :]
        for c in range(0, wg_ref.shape[1], HEAD_DIM):
            w_pair = jnp.concatenate([wg_ref[:, c:c + HEAD_DIM], wu_ref[:, c:c + HEAD_DIM]], axis=1)
            gu = jnp.dot(h, w_pair, preferred_element_type=F32)
            gate, up = gu[:, :HEAD_DIM], gu[:, HEAD_DIM:]
            act_ref[rows, c:c + HEAD_DIM] = (gate * jax.nn.sigmoid(gate) * up).astype(BF16)

    @pl.when(last)
    def _():
        x1_copy().wait()
        y = o_ref[...] + acc_ref[...]
        o_ref[...] = y * _rms_scale(y) * gfin_ref[...]


def _ffn(h2, x1, w_gate_up, w_down, g_final):
    m = h2.shape[0]
    tm, tf = TM_FFN, TF_FFN
    nf = D_FF // tf
    n = (m // tm) * nf

    def started(j):
        t = jnp.minimum(j, n - 1)
        return t // nf, t % nf

    def finished(j):
        t = jnp.maximum(j - 1, 0)
        return t // nf, t % nf

    return pl.pallas_call(
        functools.partial(_ffn_kernel, nf),
        grid=(n + 1,),
        in_specs=[
            pl.BlockSpec((tm, D_MODEL), lambda j: (started(j)[0], 0)),
            pl.BlockSpec(memory_space=pl.ANY),
            pl.BlockSpec((D_MODEL, tf), lambda j: (0, started(j)[1])),
            pl.BlockSpec((D_MODEL, tf), lambda j: (0, started(j)[1] + nf)),
            pl.BlockSpec((tf, D_MODEL), lambda j: (finished(j)[1], 0)),
            pl.BlockSpec((1, D_MODEL), lambda j: (0, 0)),
        ],
        out_specs=pl.BlockSpec((tm, D_MODEL), lambda j: (finished(j)[0], 0)),
        out_shape=jax.ShapeDtypeStruct((m, D_MODEL), F32),
        scratch_shapes=[pltpu.VMEM((tm, D_MODEL), F32), pltpu.VMEM((tm, tf), BF16),
                        pltpu.SemaphoreType.DMA(())],
        compiler_params=_params(("arbitrary",), 56),
        name="ffn",
    )(h2, x1, w_gate_up, w_gate_up, w_down, g_final)


def kernel(x, g_mix, w_in, g_q_a, g_k_a, g_out_a, g_out_b, w_out, g_ffn, w_gate_up, w_down, g_final):
    batch, seq, d = x.shape
    assert d == D_MODEL and seq % GRID_W == 0
    assert seq % TM_PROJ == 0 and seq == P4 * GROUP_TILES_B * TQ_B and seq == max(DILATIONS) * WIN_B
    depth = w_in.shape[0]
    tables = _rope_tables(seq)
    x2 = x.reshape(batch * seq, d)
    for layer in range(depth):
        qa, ka, va, qb, kb, vb = _in_projection(
            x2, g_mix[layer][None], w_in[layer].astype(BF16),
            g_q_a[layer][None], g_k_a[layer][None], tables, seq)
        oa = _mixer_a(qa, ka, va, batch, seq)
        ob = _mixer_b(qb, kb, vb, batch, seq)
        x1, h2 = _out_projection(oa, ob, x2, g_out_a[layer][None], g_out_b[layer][None],
                                 w_out[layer].astype(BF16), g_ffn[layer][None])
        assert depth == 1
        x2 = _ffn(h2, x1, w_gate_up[layer].astype(BF16), w_down[layer].astype(BF16), g_final[None])
    return x2.reshape(batch, seq, d)
```

```python
import functools
import math

import jax
import jax.numpy as jnp
import numpy as np
from jax import lax
from jax.experimental import pallas as pl
from jax.experimental.pallas import tpu as pltpu

F32 = jnp.float32
BF16 = jnp.bfloat16

D_MODEL = 2048
HEAD_DIM = 128
N_HEADS_A = 8
N_KV_A = 2
GROUP_A = N_HEADS_A // N_KV_A
N_HEADS_B = 8
WIDTH_A = N_HEADS_A * HEAD_DIM
WIDTH_B = N_HEADS_B * HEAD_DIM
KV_A = N_KV_A * HEAD_DIM
PROJ_OUT = WIDTH_A + 2 * KV_A + 3 * WIDTH_B
GRID_W = 64
AXIAL_THETA = 10000.0
ROPE_THETA = 500000.0
PARTIAL_ROT = HEAD_DIM // 4
DILATIONS = (1, 4, 16)
BAND = 64
D_FF = 5632
EPS = 1e-6
Q_SCALE = HEAD_DIM ** -0.5 * math.log2(math.e)

MIB = 1024 * 1024

TM_PROJ = 512
TQ_A = 256
TK_A = 512
CAST_ROWS = 32
TQ_B = 128
WIN_B = TQ_B + 2 * BAND
P4 = 4
GROUP_TILES_B = 8
TM_OUT = 512
TM_FFN = 1024
FFN_SUB_ROWS = 512
TF_FFN = 512


def _params(semantics, vmem_mib):
    return pltpu.CompilerParams(dimension_semantics=semantics, vmem_limit_bytes=vmem_mib * MIB)


def _rms_scale(y):
    return lax.rsqrt(jnp.mean(y * y, axis=-1, keepdims=True) + EPS)


def _rope_tables(seq):
    def angles(pos, dim, theta):
        inv = theta ** (-(np.arange(0, dim, 2, dtype=np.float64) / dim))
        return pos[:, None] * inv[None, :]

    half = HEAD_DIM // 2
    rows = seq // GRID_W
    row_pos = np.repeat(np.arange(rows, dtype=np.float64), GRID_W)
    col_pos = np.tile(np.arange(GRID_W, dtype=np.float64), rows)
    ar, ac = angles(row_pos, half, AXIAL_THETA), angles(col_pos, half, AXIAL_THETA)
    z = np.zeros_like(ar)
    cos_a = np.concatenate([np.cos(ar), np.cos(ar), np.cos(ac), np.cos(ac)], axis=-1)
    up_a = np.concatenate([-np.sin(ar), z, -np.sin(ac), z], axis=-1)
    dn_a = np.concatenate([z, np.sin(ar), z, np.sin(ac)], axis=-1)

    ab = angles(np.arange(seq, dtype=np.float64), PARTIAL_ROT, ROPE_THETA)
    hw = PARTIAL_ROT // 2
    one_rest = np.ones((seq, HEAD_DIM - PARTIAL_ROT))
    cos_b = np.concatenate([np.cos(ab), np.cos(ab), one_rest], axis=-1)
    up_b = np.concatenate([-np.sin(ab), np.zeros((seq, HEAD_DIM - hw))], axis=-1)
    dn_b = np.concatenate([np.zeros((seq, hw)), np.sin(ab), np.zeros((seq, HEAD_DIM - PARTIAL_ROT))], axis=-1)
    return tuple(jnp.asarray(t, dtype=F32) for t in (cos_a, up_a, dn_a, cos_b, up_b, dn_b))


def _rotate(t, cos, s_up, s_dn, shift):
    return (t * cos + pltpu.roll(t, HEAD_DIM - shift, 1) * s_up
            + pltpu.roll(t, shift, 1) * s_dn)


def _inproj_kernel(x_ref, gmix_ref, w_ref, gq_ref, gk_ref,
                   ca_ref, ua_ref, da_ref, cb_ref, ub_ref, db_ref,
                   qa_ref, ka_ref, va_ref, qb_ref, kb_ref, vb_ref):
    x = x_ref[...]
    h = (x * _rms_scale(x) * gmix_ref[...]).astype(BF16)
    ca, ua, da = ca_ref[...], ua_ref[...], da_ref[...]
    cb, ub, db = cb_ref[...], ub_ref[...], db_ref[...]
    gq, gk = gq_ref[...], gk_ref[...]
    ones = jnp.ones((x.shape[0], HEAD_DIM), BF16)

    pair = {}
    head_sum = (lax.broadcasted_iota(jnp.int32, (2 * HEAD_DIM, 2 * HEAD_DIM), 0) // HEAD_DIM
                == lax.broadcasted_iota(jnp.int32, (2 * HEAD_DIM, 2 * HEAD_DIM), 1) // HEAD_DIM
                ).astype(BF16)

    def head(col):
        base = col - col % (2 * HEAD_DIM)
        if base not in pair:
            pair.clear()
            pair[base] = jnp.dot(h, w_ref[:, base:base + 2 * HEAD_DIM], preferred_element_type=F32)
        return pair[base][:, col - base:col - base + HEAD_DIM]

    qa0, ka0, va0 = 0, WIDTH_A, WIDTH_A + KV_A
    qb0 = WIDTH_A + 2 * KV_A
    kb0, vb0 = qb0 + WIDTH_B, qb0 + 2 * WIDTH_B
    for i in range(N_HEADS_B):
        qb_ref[i] = _rotate(head(qb0 + i * HEAD_DIM), cb, ub, db, PARTIAL_ROT // 2) * Q_SCALE
    for i in range(N_HEADS_B):
        kb_ref[i] = _rotate(head(kb0 + i * HEAD_DIM), cb, ub, db, PARTIAL_ROT // 2)

    ya = [head(c) for c in range(qa0, va0, HEAD_DIM)]
    z = jnp.concatenate([jnp.concatenate([ya[i] * ya[i], ya[i + 1] * ya[i + 1]], axis=1)
                         for i in range(0, len(ya), 2)], axis=0)
    z_hi = z.astype(BF16)
    z_lo = (z - z_hi.astype(F32)).astype(BF16)
    ssq = (jnp.dot(z_hi, head_sum, preferred_element_type=F32)
           + jnp.dot(z_lo, head_sum, preferred_element_type=F32))
    r = lax.rsqrt(ssq * (1.0 / HEAD_DIM) + EPS)
    tm = x.shape[0]
    for i, y in enumerate(ya):
        g, scale = (gq, Q_SCALE) if i < N_HEADS_A else (gk, 1.0)
        ri = r[(i // 2) * tm:(i // 2 + 1) * tm, (i % 2) * HEAD_DIM:(i % 2 + 1) * HEAD_DIM]
        out = (_rotate(y * g, ca, ua, da, HEAD_DIM // 4) * (ri * scale)).astype(BF16)
        if i < N_HEADS_A:
            qa_ref[i] = out
        else:
            ka_ref[i - N_HEADS_A] = out
    for i in range(N_KV_A):
        va_ref[i, :, :HEAD_DIM] = head(va0 + i * HEAD_DIM).astype(BF16)
        va_ref[i, :, HEAD_DIM:] = ones
    for i in range(N_HEADS_B):
        vb_ref[i] = head(vb0 + i * HEAD_DIM)


def _in_projection(x2, g_mix, w_in, g_q, g_k, tables, seq):
    m = x2.shape[0]
    tm = TM_PROJ
    steps_per_seq = seq // tm
    row = lambda i: (i, 0)
    fixed = lambda i: (0, 0)
    tab = pl.BlockSpec((tm, HEAD_DIM), lambda i: (i % steps_per_seq, 0))
    heads = lambda n: pl.BlockSpec((n, tm, HEAD_DIM), lambda i: (0, i, 0))
    return pl.pallas_call(
        _inproj_kernel,
        grid=(m // tm,),
        in_specs=[
            pl.BlockSpec((tm, D_MODEL), row),
            pl.BlockSpec((1, D_MODEL), fixed),
            pl.BlockSpec((D_MODEL, PROJ_OUT), fixed, pipeline_mode=pl.Buffered(1)),
            pl.BlockSpec((1, HEAD_DIM), fixed),
            pl.BlockSpec((1, HEAD_DIM), fixed),
            tab, tab, tab, tab, tab, tab,
        ],
        out_specs=[
            heads(N_HEADS_A),
            heads(N_KV_A),
            pl.BlockSpec((N_KV_A, tm, 2 * HEAD_DIM), lambda i: (0, i, 0)),
            heads(N_HEADS_B), heads(N_HEADS_B), heads(N_HEADS_B),
        ],
        out_shape=[
            jax.ShapeDtypeStruct((N_HEADS_A, m, HEAD_DIM), BF16),
            jax.ShapeDtypeStruct((N_KV_A, m, HEAD_DIM), BF16),
            jax.ShapeDtypeStruct((N_KV_A, m, 2 * HEAD_DIM), BF16),
            jax.ShapeDtypeStruct((N_HEADS_B, m, HEAD_DIM), F32),
            jax.ShapeDtypeStruct((N_HEADS_B, m, HEAD_DIM), F32),
            jax.ShapeDtypeStruct((N_HEADS_B, m, HEAD_DIM), F32),
        ],
        compiler_params=_params(("parallel",), 56),
        name="in_projection",
    )(x2, g_mix, w_in, g_q, g_k, *tables)


def _mixer_a_kernel(cast_blocks, q_ref, k_ref, v_ref, *refs):
    nw = len(cast_blocks)
    w_src, o_ref, w_dst, (s_ref, m_ref) = refs[:nw], refs[nw], refs[nw + 1:2 * nw + 1], refs[2 * nw + 1:]
    tq = q_ref.shape[1]
    seq = k_ref.shape[1]
    step = pl.program_id(0)

    @pl.when(step == 0)
    def _():
        s_ref[...] = jnp.zeros(s_ref.shape, F32)
        m_ref[...] = jnp.zeros(m_ref.shape, F32)

    for src, dst, blocks in zip(w_src, w_dst, cast_blocks):
        @pl.when(step < blocks)
        def _(src=src, dst=dst):
            dst[...] = src[...].astype(BF16)

    q = jnp.concatenate([q_ref[h] for h in range(GROUP_A)], axis=0)
    m_done = m_ref[:, :1]
    acc = None
    m_run = None
    for c in range(0, seq, TK_A):
        p = jnp.exp2(s_ref[:, c:c + TK_A] - m_done).astype(BF16)
        pv = jnp.dot(p, v_ref[0, c:c + TK_A, :], preferred_element_type=F32)
        acc = pv if acc is None else acc + pv
        s = lax.dot_general(q, k_ref[0, c:c + TK_A, :], (((1,), (1,)), ((), ())),
                            preferred_element_type=F32)
        s_ref[:, c:c + TK_A] = s
        for l in range(0, TK_A, HEAD_DIM):
            part = s[:, l:l + HEAD_DIM]
            m_run = part if m_run is None else jnp.maximum(m_run, part)
    m_ref[...] = jnp.broadcast_to(jnp.max(m_run, axis=-1, keepdims=True), m_ref.shape)
    o = acc[:, :HEAD_DIM] / acc[:, HEAD_DIM:]
    for h in range(GROUP_A):
        o_ref[:, h * HEAD_DIM:(h + 1) * HEAD_DIM] = o[h * tq:(h + 1) * tq].astype(o_ref.dtype)


def _mixer_a(qa, ka, va, batch, seq, weights):
    m = qa.shape[1]
    tq = TQ_A
    tiles = seq // tq
    gw = GROUP_A * HEAD_DIM
    n = batch * N_KV_A * tiles
    cast_blocks = tuple(w.shape[0] // CAST_ROWS for w in weights)
    assert all(w.shape[0] % CAST_ROWS == 0 for w in weights) and max(cast_blocks) <= n + 1
    w_specs = [pl.BlockSpec((CAST_ROWS, w.shape[1]), lambda j, nb=nb: (jnp.minimum(j, nb - 1), 0))
               for w, nb in zip(weights, cast_blocks)]

    def coords(t):
        bg, i = t // tiles, t % tiles
        b, g = bg // N_KV_A, bg % N_KV_A
        return b * tiles + i, g, b

    def started(j):
        return coords(jnp.minimum(j, n - 1))

    def finished(j):
        return coords(jnp.maximum(j - 1, 0))

    oa, *w_bf16 = pl.pallas_call(
        functools.partial(_mixer_a_kernel, cast_blocks),
        grid=(n + 1,),
        in_specs=[
            pl.BlockSpec((GROUP_A, tq, HEAD_DIM), lambda j: (started(j)[1], started(j)[0], 0)),
            pl.BlockSpec((1, seq, HEAD_DIM), lambda j: (started(j)[1], started(j)[2], 0)),
            pl.BlockSpec((1, seq, 2 * HEAD_DIM), lambda j: (finished(j)[1], finished(j)[2], 0)),
        ] + w_specs,
        out_specs=[pl.BlockSpec((tq, gw), lambda j: finished(j)[:2])] + w_specs,
        out_shape=[jax.ShapeDtypeStruct((m, WIDTH_A), BF16)]
        + [jax.ShapeDtypeStruct(w.shape, BF16) for w in weights],
        scratch_shapes=[pltpu.VMEM((GROUP_A * tq, seq), F32),
                        pltpu.VMEM((GROUP_A * tq, HEAD_DIM), F32)],
        compiler_params=_params(("arbitrary",), 48),
        name="mixer_a",
    )(qa, ka, va, *weights)
    return oa, w_bf16


def _mixer_b_bias(seq):
    i = np.arange(TQ_B)[:, None]
    j = np.arange(WIN_B)[None, :]
    q4, k4 = TQ_B // P4, WIN_B // P4
    kinds = [P4 * (off + i % q4 - j % k4) + i // q4 - j // k4 for off in (0, BAND // P4, 2 * BAND // P4)]
    kinds += [off + i - j for off in (0, BAND, 2 * BAND)]
    kinds += [off + i - j for off in (0, TQ_B)]
    return jnp.asarray(np.stack([np.where(np.abs(dist) <= BAND, 0.0, -np.inf) for dist in kinds]), dtype=F32)


def _mixer_b_kernel(q_ref, k_ref, v_ref, bias_ref, o_ref,
                    qf_ref, kf_ref, vf_ref, qb_ref, kb_ref, vb_ref, s_ref, m_ref, part_ref, lse_ref, merged_ref):
    seq = q_ref.shape[0]
    chunk = seq // P4
    g_tiles = s_ref.shape[0]
    n_groups = seq // TQ_B // g_tiles
    assert chunk == g_tiles * TQ_B and chunk == P4 * WIN_B

    vb_ref[:, HEAD_DIM:] = jnp.ones((seq, HEAD_DIM), BF16)
    for r in range(P4):
        dst = slice(r * chunk, (r + 1) * chunk)
        for src, f_ref, b_ref in ((q_ref, qf_ref, qb_ref), (k_ref, kf_ref, kb_ref), (v_ref, vf_ref, vb_ref)):
            x = src[pl.ds(r, chunk, stride=P4), :]
            f_ref[dst, :] = x
            b_ref[dst, :HEAD_DIM] = x.astype(BF16)

    def pieces(ref, start, rows):
        return jnp.concatenate([ref[pl.ds(r * chunk + start, rows), :] for r in range(P4)], axis=0)

    def tile_rows(branch, g, t):
        if branch == 0:
            q0 = pl.multiple_of(g * (g_tiles * TQ_B // P4) + t * (TQ_B // P4), TQ_B // P4)
            k0 = pl.multiple_of(jnp.clip(q0 - BAND // P4, 0, chunk - WIN_B // P4), BAND // P4)
            kind = (q0 - k0) // (BAND // P4)

            def put(part, lse):
                for ref, val in ((part_ref, part), (lse_ref, lse)):
                    for r in range(P4):
                        ref[0, pl.ds(r * chunk + q0, TQ_B // P4), :] = val[r * (TQ_B // P4):(r + 1) * (TQ_B // P4)]
            return (lambda: pieces(qb_ref, q0, TQ_B // P4), lambda: pieces(kb_ref, k0, WIN_B // P4),
                    lambda: pieces(vb_ref, k0, WIN_B // P4), kind, put)
        base = pl.multiple_of(g * chunk, chunk)
        if branch == 1:
            q0 = t * TQ_B
            k0 = min(max(q0 - BAND, 0), chunk - WIN_B)
            kind = 3 + (q0 - k0) // BAND

            def put(part, lse):
                part_ref[1, pl.ds(base + q0, TQ_B), :] = part
                lse_ref[1, pl.ds(base + q0, TQ_B), :] = lse
            return (lambda: qb_ref[pl.ds(base + q0, TQ_B), :], lambda: kb_ref[pl.ds(base + k0, WIN_B), :],
                    lambda: vb_ref[pl.ds(base + k0, WIN_B), :], kind, put)
        a, u = t // 2, t % 2
        q_rows = pl.ds(base + a + P4 * TQ_B * u, TQ_B, stride=P4)
        k_rows = pl.ds(base + a, WIN_B, stride=P4)

        def put(part, lse):
            parts = [part_ref[0, q_rows, :], part_ref[1, q_rows, :], part]
            lses = [lse_ref[0, q_rows, :], lse_ref[1, q_rows, :], lse]
            top = jnp.maximum(jnp.maximum(lses[0], lses[1]), lses[2])
            w = [jnp.exp2(l - top) for l in lses]
            num = w[0] * parts[0] + w[1] * parts[1] + w[2] * parts[2]
            merged_ref[q_rows, :] = num / (w[0] + w[1] + w[2])
        return (lambda: qf_ref[q_rows, :].astype(BF16), lambda: kf_ref[k_rows, :].astype(BF16),
                lambda: jnp.concatenate([vf_ref[k_rows, :].astype(BF16), jnp.ones((WIN_B, HEAD_DIM), BF16)], axis=1),
                6 + u, put)

    def start(branch, g, t):
        q, k, _, kind, _ = tile_rows(branch, g, t)
        s = lax.dot_general(q(), k(), (((1,), (1,)), ((), ())), preferred_element_type=F32) + bias_ref[kind]
        s_ref[t] = s
        m_ref[t] = jnp.broadcast_to(jnp.max(s, axis=-1, keepdims=True), (TQ_B, HEAD_DIM))

    def finish(branch, g, t):
        _, _, v, _, put = tile_rows(branch, g, t)
        m = m_ref[t]
        p = jnp.concatenate([jnp.exp2(s_ref[t, :, c:c + HEAD_DIM] - m) for c in range(0, WIN_B, HEAD_DIM)],
                            axis=1).astype(BF16)
        pv = jnp.dot(p, v(), preferred_element_type=F32)
        den = pv[:, HEAD_DIM:]
        put(pv[:, :HEAD_DIM] / den, m + jnp.log2(den))

    for t in range(g_tiles):
        start(0, 0, t)
    for branch in range(len(DILATIONS)):
        def body(g, carry, branch=branch):
            for t in range(g_tiles):
                finish(branch, g - 1, t)
                start(branch, g, t)
            return carry

        lax.fori_loop(1, n_groups, body, 0)
        for t in range(g_tiles):
            finish(branch, n_groups - 1, t)
            if branch + 1 < len(DILATIONS):
                start(branch + 1, 0, t)

    for r in range(P4):
        qf_ref[pl.ds(r, chunk, stride=P4), :] = merged_ref[r * chunk:(r + 1) * chunk, :]
    o_ref[...] = qf_ref[...].astype(o_ref.dtype)


def _mixer_b(qb, kb, vb, batch, seq):
    m = qb.shape[1]
    spec = pl.BlockSpec((None, seq, HEAD_DIM), lambda b, h: (h, b, 0))
    bias = _mixer_b_bias(seq)
    f32_rows = pltpu.VMEM((seq, HEAD_DIM), F32)
    bf16_rows = pltpu.VMEM((seq, HEAD_DIM), BF16)
    stored_branches = pltpu.VMEM((len(DILATIONS) - 1, seq, HEAD_DIM), F32)
    return pl.pallas_call(
        _mixer_b_kernel,
        grid=(batch, N_HEADS_B),
        in_specs=[spec, spec, spec, pl.BlockSpec(bias.shape, lambda b, h: (0, 0, 0))],
        out_specs=pl.BlockSpec((seq, HEAD_DIM), lambda b, h: (b, h)),
        out_shape=jax.ShapeDtypeStruct((m, WIDTH_B), BF16),
        scratch_shapes=[
            f32_rows, f32_rows, f32_rows,
            bf16_rows, bf16_rows, pltpu.VMEM((seq, 2 * HEAD_DIM), BF16),
            pltpu.VMEM((GROUP_TILES_B, TQ_B, WIN_B), F32),
            pltpu.VMEM((GROUP_TILES_B, TQ_B, HEAD_DIM), F32),
            stored_branches, stored_branches, f32_rows,
        ],
        compiler_params=_params(("parallel", "parallel"), 56),
        name="mixer_b",
    )(qb, kb, vb, bias)


def _outproj_kernel(oa_ref, ob_ref, x_ref, ga_ref, gb_ref, w_ref, gffn_ref, x1_ref, h2_ref):
    oa, ob = oa_ref[...].astype(F32), ob_ref[...].astype(F32)
    na = (oa * _rms_scale(oa) * ga_ref[...]).astype(BF16)
    nb = (ob * _rms_scale(ob) * gb_ref[...]).astype(BF16)
    x1 = (x_ref[...]
          + jnp.dot(na, w_ref[:WIDTH_A, :], preferred_element_type=F32)
          + jnp.dot(nb, w_ref[WIDTH_A:, :], preferred_element_type=F32))
    x1_ref[...] = x1
    h2_ref[...] = (x1 * _rms_scale(x1) * gffn_ref[...]).astype(BF16)


def _out_projection(oa, ob, x2, g_a, g_b, w_out, g_ffn):
    m = x2.shape[0]
    tm = TM_OUT
    row = lambda i: (i, 0)
    fixed = lambda i: (0, 0)
    return pl.pallas_call(
        _outproj_kernel,
        grid=(m // tm,),
        in_specs=[
            pl.BlockSpec((tm, WIDTH_A), row),
            pl.BlockSpec((tm, WIDTH_B), row),
            pl.BlockSpec((tm, D_MODEL), row),
            pl.BlockSpec((1, WIDTH_A), fixed),
            pl.BlockSpec((1, WIDTH_B), fixed),
            pl.BlockSpec((WIDTH_A + WIDTH_B, D_MODEL), fixed, pipeline_mode=pl.Buffered(1)),
            pl.BlockSpec((1, D_MODEL), fixed),
        ],
        out_specs=[pl.BlockSpec((tm, D_MODEL), row), pl.BlockSpec((tm, D_MODEL), row)],
        out_shape=[jax.ShapeDtypeStruct((m, D_MODEL), F32), jax.ShapeDtypeStruct((m, D_MODEL), BF16)],
        compiler_params=_params(("parallel",), 40),
        name="out_projection",
    )(oa, ob, x2, g_a, g_b, w_out, g_ffn)


def _ffn_kernel(nf, h_ref, x1_hbm, wg_ref, wu_ref, wd_ref, gfin_ref, o_ref, acc_ref, act_ref, x1_sem):
    j = pl.program_id(0)
    tm = acc_ref.shape[0]
    t_done = jnp.maximum(j - 1, 0)
    f_done = t_done % nf
    first, last = jnp.logical_and(j > 0, f_done == 0), jnp.logical_and(j > 0, f_done == nf - 1)

    def x1_copy():
        return pltpu.make_async_copy(x1_hbm.at[pl.ds((t_done // nf) * tm, tm), :], o_ref, x1_sem)

    @pl.when(j == 0)
    def _():
        act_ref[...] = jnp.zeros(act_ref.shape, BF16)
        acc_ref[...] = jnp.zeros(acc_ref.shape, F32)

    @pl.when(first)
    def _():
        x1_copy().start()

    for r in range(0, tm, FFN_SUB_ROWS):
        rows = slice(r, r + FFN_SUB_ROWS)
        part = jnp.dot(act_ref[rows, :], wd_ref[...], preferred_element_type=F32)
        acc_ref[rows, :] = jnp.where(f_done == 0, part, acc_ref[rows, :] + part)
        h = h_ref[rows, :]
        for c in range(0, wg_ref.shape[1], HEAD_DIM):
            w_pair = jnp.concatenate([wg_ref[:, c:c + HEAD_DIM], wu_ref[:, c:c + HEAD_DIM]], axis=1)
            gu = jnp.dot(h, w_pair, preferred_element_type=F32)
            gate, up = gu[:, :HEAD_DIM], gu[:, HEAD_DIM:]
            act_ref[rows, c:c + HEAD_DIM] = (gate * jax.nn.sigmoid(gate) * up).astype(BF16)

    @pl.when(last)
    def _():
        x1_copy().wait()
        y = o_ref[...] + acc_ref[...]
        o_ref[...] = y * _rms_scale(y) * gfin_ref[...]


def _ffn(h2, x1, w_gate_up, w_down, g_final):
    m = h2.shape[0]
    tm, tf = TM_FFN, TF_FFN
    nf = D_FF // tf
    n = (m // tm) * nf

    def started(j):
        t = jnp.minimum(j, n - 1)
        return t // nf, t % nf

    def finished(j):
        t = jnp.maximum(j - 1, 0)
        return t // nf, t % nf

    return pl.pallas_call(
        functools.partial(_ffn_kernel, nf),
        grid=(n + 1,),
        in_specs=[
            pl.BlockSpec((tm, D_MODEL), lambda j: (started(j)[0], 0)),
            pl.BlockSpec(memory_space=pl.ANY),
            pl.BlockSpec((D_MODEL, tf), lambda j: (0, started(j)[1])),
            pl.BlockSpec((D_MODEL, tf), lambda j: (0, started(j)[1] + nf)),
            pl.BlockSpec((tf, D_MODEL), lambda j: (finished(j)[1], 0)),
            pl.BlockSpec((1, D_MODEL), lambda j: (0, 0)),
        ],
        out_specs=pl.BlockSpec((tm, D_MODEL), lambda j: (finished(j)[0], 0)),
        out_shape=jax.ShapeDtypeStruct((m, D_MODEL), F32),
        scratch_shapes=[pltpu.VMEM((tm, D_MODEL), F32), pltpu.VMEM((tm, tf), BF16),
                        pltpu.SemaphoreType.DMA(())],
        compiler_params=_params(("arbitrary",), 56),
        name="ffn",
    )(h2, x1, w_gate_up, w_gate_up, w_down, g_final)


def kernel(x, g_mix, w_in, g_q_a, g_k_a, g_out_a, g_out_b, w_out, g_ffn, w_gate_up, w_down, g_final):
    batch, seq, d = x.shape
    assert d == D_MODEL and seq % GRID_W == 0
    assert seq % TM_PROJ == 0 and seq == P4 * GROUP_TILES_B * TQ_B and seq == max(DILATIONS) * WIN_B
    depth = w_in.shape[0]
    tables = _rope_tables(seq)
    x2 = x.reshape(batch * seq, d)
    for layer in range(depth):
        qa, ka, va, qb, kb, vb = _in_projection(
            x2, g_mix[layer][None], w_in[layer].astype(BF16),
            g_q_a[layer][None], g_k_a[layer][None], tables, seq)
        oa, (w_out_b, w_gate_up_b, w_down_b) = _mixer_a(
            qa, ka, va, batch, seq, (w_out[layer], w_gate_up[layer], w_down[layer]))
        ob = _mixer_b(qb, kb, vb, batch, seq)
        x1, h2 = _out_projection(oa, ob, x2, g_out_a[layer][None], g_out_b[layer][None],
                                 w_out_b, g_ffn[layer][None])
        assert depth == 1
        x2 = _ffn(h2, x1, w_gate_up_b, w_down_b, g_final[None])
    return x2.reshape(batch, seq, d)
```

```python
import functools
import math

import jax
import jax.numpy as jnp
import numpy as np
from jax import lax
from jax.experimental import pallas as pl
from jax.experimental.pallas import tpu as pltpu

F32 = jnp.float32
BF16 = jnp.bfloat16

D_MODEL = 2048
HEAD_DIM = 128
N_HEADS_A = 8
N_KV_A = 2
GROUP_A = N_HEADS_A // N_KV_A
N_HEADS_B = 8
WIDTH_A = N_HEADS_A * HEAD_DIM
WIDTH_B = N_HEADS_B * HEAD_DIM
KV_A = N_KV_A * HEAD_DIM
PROJ_OUT = WIDTH_A + 2 * KV_A + 3 * WIDTH_B
GRID_W = 64
AXIAL_THETA = 10000.0
ROPE_THETA = 500000.0
PARTIAL_ROT = HEAD_DIM // 4
DILATIONS = (1, 4, 16)
BAND = 64
D_FF = 5632
EPS = 1e-6
Q_SCALE = HEAD_DIM ** -0.5 * math.log2(math.e)

MIB = 1024 * 1024

TM_PROJ = 512
TQ_A = 256
TK_A = 512
BF16_TILE_ROWS = 16
TQ_B = 128
WIN_B = TQ_B + 2 * BAND
P4 = 4
GROUP_TILES_B = 8
TM_OUT = 512
TM_FFN = 1024
FFN_SUB_ROWS = 512
FFN_NORM_ROWS = 16
TF_FFN = 512


def _params(semantics, vmem_mib):
    return pltpu.CompilerParams(dimension_semantics=semantics, vmem_limit_bytes=vmem_mib * MIB)


def _rms_scale(y):
    return lax.rsqrt(jnp.mean(y * y, axis=-1, keepdims=True) + EPS)


def _rope_tables(seq):
    def angles(pos, dim, theta):
        inv = theta ** (-(np.arange(0, dim, 2, dtype=np.float64) / dim))
        return pos[:, None] * inv[None, :]

    half = HEAD_DIM // 2
    rows = seq // GRID_W
    row_pos = np.repeat(np.arange(rows, dtype=np.float64), GRID_W)
    col_pos = np.tile(np.arange(GRID_W, dtype=np.float64), rows)
    ar, ac = angles(row_pos, half, AXIAL_THETA), angles(col_pos, half, AXIAL_THETA)
    z = np.zeros_like(ar)
    cos_a = np.concatenate([np.cos(ar), np.cos(ar), np.cos(ac), np.cos(ac)], axis=-1)
    up_a = np.concatenate([-np.sin(ar), z, -np.sin(ac), z], axis=-1)
    dn_a = np.concatenate([z, np.sin(ar), z, np.sin(ac)], axis=-1)

    ab = angles(np.arange(seq, dtype=np.float64), PARTIAL_ROT, ROPE_THETA)
    hw = PARTIAL_ROT // 2
    one_rest = np.ones((seq, HEAD_DIM - PARTIAL_ROT))
    cos_b = np.concatenate([np.cos(ab), np.cos(ab), one_rest], axis=-1)
    up_b = np.concatenate([-np.sin(ab), np.zeros((seq, HEAD_DIM - hw))], axis=-1)
    dn_b = np.concatenate([np.zeros((seq, hw)), np.sin(ab), np.zeros((seq, HEAD_DIM - PARTIAL_ROT))], axis=-1)
    return tuple(jnp.asarray(t, dtype=F32) for t in (cos_a, up_a, dn_a, cos_b, up_b, dn_b))


def _rotate(t, cos, s_up, s_dn, shift):
    return (t * cos + pltpu.roll(t, HEAD_DIM - shift, 1) * s_up
            + pltpu.roll(t, shift, 1) * s_dn)


def _inproj_kernel(x_ref, gmix_ref, w_ref, gq_ref, gk_ref,
                   ca_ref, ua_ref, da_ref, cb_ref, ub_ref, db_ref,
                   qa_ref, ka_ref, va_ref, qb_ref, kb_ref, vb_ref):
    x = x_ref[...]
    h = (x * _rms_scale(x) * gmix_ref[...]).astype(BF16)
    ca, ua, da = ca_ref[...], ua_ref[...], da_ref[...]
    cb, ub, db = cb_ref[...], ub_ref[...], db_ref[...]
    gq, gk = gq_ref[...], gk_ref[...]
    ones = jnp.ones((x.shape[0], HEAD_DIM), BF16)

    pair = {}
    head_sum = (lax.broadcasted_iota(jnp.int32, (2 * HEAD_DIM, 2 * HEAD_DIM), 0) // HEAD_DIM
                == lax.broadcasted_iota(jnp.int32, (2 * HEAD_DIM, 2 * HEAD_DIM), 1) // HEAD_DIM
                ).astype(BF16)

    def head(col):
        base = col - col % (2 * HEAD_DIM)
        if base not in pair:
            pair.clear()
            pair[base] = jnp.dot(h, w_ref[:, base:base + 2 * HEAD_DIM], preferred_element_type=F32)
        return pair[base][:, col - base:col - base + HEAD_DIM]

    qa0, ka0, va0 = 0, WIDTH_A, WIDTH_A + KV_A
    qb0 = WIDTH_A + 2 * KV_A
    kb0, vb0 = qb0 + WIDTH_B, qb0 + 2 * WIDTH_B
    for i in range(N_HEADS_B):
        qb_ref[i] = _rotate(head(qb0 + i * HEAD_DIM), cb, ub, db, PARTIAL_ROT // 2) * Q_SCALE
    for i in range(N_HEADS_B):
        kb_ref[i] = _rotate(head(kb0 + i * HEAD_DIM), cb, ub, db, PARTIAL_ROT // 2)

    ya = [head(c) for c in range(qa0, va0, HEAD_DIM)]
    z = jnp.concatenate([jnp.concatenate([ya[i] * ya[i], ya[i + 1] * ya[i + 1]], axis=1)
                         for i in range(0, len(ya), 2)], axis=0)
    z_hi = z.astype(BF16)
    z_lo = (z - z_hi.astype(F32)).astype(BF16)
    ssq = (jnp.dot(z_hi, head_sum, preferred_element_type=F32)
           + jnp.dot(z_lo, head_sum, preferred_element_type=F32))
    r = lax.rsqrt(ssq * (1.0 / HEAD_DIM) + EPS)
    tm = x.shape[0]
    for i, y in enumerate(ya):
        g, scale = (gq, Q_SCALE) if i < N_HEADS_A else (gk, 1.0)
        ri = r[(i // 2) * tm:(i // 2 + 1) * tm, (i % 2) * HEAD_DIM:(i % 2 + 1) * HEAD_DIM]
        out = (_rotate(y * g, ca, ua, da, HEAD_DIM // 4) * (ri * scale)).astype(BF16)
        if i < N_HEADS_A:
            qa_ref[i] = out
        else:
            ka_ref[i - N_HEADS_A] = out
    for i in range(N_KV_A):
        va_ref[i, :, :HEAD_DIM] = head(va0 + i * HEAD_DIM).astype(BF16)
        va_ref[i, :, HEAD_DIM:] = ones
    for i in range(N_HEADS_B):
        vb_ref[i] = head(vb0 + i * HEAD_DIM)


def _in_projection(x2, g_mix, w_in, g_q, g_k, tables, seq):
    m = x2.shape[0]
    tm = TM_PROJ
    steps_per_seq = seq // tm
    row = lambda i: (i, 0)
    fixed = lambda i: (0, 0)
    tab = pl.BlockSpec((tm, HEAD_DIM), lambda i: (i % steps_per_seq, 0))
    heads = lambda n: pl.BlockSpec((n, tm, HEAD_DIM), lambda i: (0, i, 0))
    return pl.pallas_call(
        _inproj_kernel,
        grid=(m // tm,),
        in_specs=[
            pl.BlockSpec((tm, D_MODEL), row),
            pl.BlockSpec((1, D_MODEL), fixed),
            pl.BlockSpec((D_MODEL, PROJ_OUT), fixed, pipeline_mode=pl.Buffered(1)),
            pl.BlockSpec((1, HEAD_DIM), fixed),
            pl.BlockSpec((1, HEAD_DIM), fixed),
            tab, tab, tab, tab, tab, tab,
        ],
        out_specs=[
            heads(N_HEADS_A),
            heads(N_KV_A),
            pl.BlockSpec((N_KV_A, tm, 2 * HEAD_DIM), lambda i: (0, i, 0)),
            heads(N_HEADS_B), heads(N_HEADS_B), heads(N_HEADS_B),
        ],
        out_shape=[
            jax.ShapeDtypeStruct((N_HEADS_A, m, HEAD_DIM), BF16),
            jax.ShapeDtypeStruct((N_KV_A, m, HEAD_DIM), BF16),
            jax.ShapeDtypeStruct((N_KV_A, m, 2 * HEAD_DIM), BF16),
            jax.ShapeDtypeStruct((N_HEADS_B, m, HEAD_DIM), F32),
            jax.ShapeDtypeStruct((N_HEADS_B, m, HEAD_DIM), F32),
            jax.ShapeDtypeStruct((N_HEADS_B, m, HEAD_DIM), F32),
        ],
        compiler_params=_params(("parallel",), 56),
        name="in_projection",
    )(x2, g_mix, w_in, g_q, g_k, *tables)


def _mixer_a_kernel(cast_blocks, q_ref, k_ref, v_ref, *refs):
    nw = len(cast_blocks)
    w_src, o_ref, w_dst, (s_ref, m_ref) = refs[:nw], refs[nw], refs[nw + 1:2 * nw + 1], refs[2 * nw + 1:]
    tq = q_ref.shape[1]
    seq = k_ref.shape[1]
    step = pl.program_id(0)

    @pl.when(step == 0)
    def _():
        s_ref[...] = jnp.zeros(s_ref.shape, F32)
        m_ref[...] = jnp.zeros(m_ref.shape, F32)

    for src, dst, blocks in zip(w_src, w_dst, cast_blocks):
        @pl.when(step < blocks)
        def _(src=src, dst=dst):
            dst[...] = src[...].astype(BF16)

    q = jnp.concatenate([q_ref[h] for h in range(GROUP_A)], axis=0)
    m_done = m_ref[...]
    acc = None
    m_run = None
    for c in range(0, seq, TK_A):
        p = jnp.concatenate([jnp.exp2(s_ref[:, l:l + HEAD_DIM] - m_done) for l in range(c, c + TK_A, HEAD_DIM)],
                            axis=1).astype(BF16)
        pv = jnp.dot(p, v_ref[0, c:c + TK_A, :], preferred_element_type=F32)
        acc = pv if acc is None else acc + pv
        s = lax.dot_general(q, k_ref[0, c:c + TK_A, :], (((1,), (1,)), ((), ())),
                            preferred_element_type=F32)
        s_ref[:, c:c + TK_A] = s
        for l in range(0, TK_A, HEAD_DIM):
            part = s[:, l:l + HEAD_DIM]
            m_run = part if m_run is None else jnp.maximum(m_run, part)
    m_ref[...] = jnp.broadcast_to(jnp.max(m_run, axis=-1, keepdims=True), m_ref.shape)
    o = acc[:, :HEAD_DIM] / acc[:, HEAD_DIM:]
    for h in range(GROUP_A):
        o_ref[:, h * HEAD_DIM:(h + 1) * HEAD_DIM] = o[h * tq:(h + 1) * tq].astype(o_ref.dtype)


def _mixer_a(qa, ka, va, batch, seq, weights):
    m = qa.shape[1]
    tq = TQ_A
    tiles = seq // tq
    gw = GROUP_A * HEAD_DIM
    n = batch * N_KV_A * tiles
    def cast_rows(w):
        return next(r for r in range(BF16_TILE_ROWS, w.shape[0] + 1, BF16_TILE_ROWS)
                    if w.shape[0] % r == 0 and w.shape[0] // r <= n + 1)

    cast_blocks = tuple(w.shape[0] // cast_rows(w) for w in weights)
    w_specs = [pl.BlockSpec((cast_rows(w), w.shape[1]), lambda j, nb=nb: (jnp.minimum(j, nb - 1), 0))
               for w, nb in zip(weights, cast_blocks)]

    def coords(t):
        bg, i = t // tiles, t % tiles
        b, g = bg // N_KV_A, bg % N_KV_A
        return b * tiles + i, g, b

    def started(j):
        return coords(jnp.minimum(j, n - 1))

    def finished(j):
        return coords(jnp.maximum(j - 1, 0))

    oa, *w_bf16 = pl.pallas_call(
        functools.partial(_mixer_a_kernel, cast_blocks),
        grid=(n + 1,),
        in_specs=[
            pl.BlockSpec((GROUP_A, tq, HEAD_DIM), lambda j: (started(j)[1], started(j)[0], 0)),
            pl.BlockSpec((1, seq, HEAD_DIM), lambda j: (started(j)[1], started(j)[2], 0)),
            pl.BlockSpec((1, seq, 2 * HEAD_DIM), lambda j: (finished(j)[1], finished(j)[2], 0)),
        ] + w_specs,
        out_specs=[pl.BlockSpec((tq, gw), lambda j: finished(j)[:2])] + w_specs,
        out_shape=[jax.ShapeDtypeStruct((m, WIDTH_A), BF16)]
        + [jax.ShapeDtypeStruct(w.shape, BF16) for w in weights],
        scratch_shapes=[pltpu.VMEM((GROUP_A * tq, seq), F32),
                        pltpu.VMEM((GROUP_A * tq, HEAD_DIM), F32)],
        compiler_params=_params(("arbitrary",), 48),
        name="mixer_a",
    )(qa, ka, va, *weights)
    return oa, w_bf16


def _mixer_b_bias(seq):
    i = np.arange(TQ_B)[:, None]
    j = np.arange(WIN_B)[None, :]
    q4, k4 = TQ_B // P4, WIN_B // P4
    kinds = [P4 * (off + i % q4 - j % k4) + i // q4 - j // k4 for off in (0, BAND // P4, 2 * BAND // P4)]
    kinds += [off + i - j for off in (0, BAND, 2 * BAND)]
    kinds += [off + i - j for off in (0, TQ_B)]
    return jnp.asarray(np.stack([np.where(np.abs(dist) <= BAND, 0.0, -np.inf) for dist in kinds]), dtype=F32)


def _mixer_b_kernel(q_ref, k_ref, v_ref, bias_ref, o_ref,
                    qf_ref, kf_ref, vf_ref, qb_ref, kb_ref, vb_ref, s_ref, m_ref, part_ref, lse_ref, merged_ref):
    seq = q_ref.shape[0]
    chunk = seq // P4
    g_tiles = s_ref.shape[0]
    n_groups = seq // TQ_B // g_tiles
    assert chunk == g_tiles * TQ_B and chunk == P4 * WIN_B

    vb_ref[:, HEAD_DIM:] = jnp.ones((seq, HEAD_DIM), BF16)
    for r in range(P4):
        dst = slice(r * chunk, (r + 1) * chunk)
        for src, f_ref, b_ref in ((q_ref, qf_ref, qb_ref), (k_ref, kf_ref, kb_ref), (v_ref, vf_ref, vb_ref)):
            x = src[pl.ds(r, chunk, stride=P4), :]
            f_ref[dst, :] = x
            b_ref[dst, :HEAD_DIM] = x.astype(BF16)

    def pieces(ref, start, rows):
        return jnp.concatenate([ref[pl.ds(r * chunk + start, rows), :] for r in range(P4)], axis=0)

    def tile_rows(branch, g, t):
        if branch == 0:
            q0 = pl.multiple_of(g * (g_tiles * TQ_B // P4) + t * (TQ_B // P4), TQ_B // P4)
            k0 = pl.multiple_of(jnp.clip(q0 - BAND // P4, 0, chunk - WIN_B // P4), BAND // P4)
            kind = (q0 - k0) // (BAND // P4)

            def put(part, lse):
                for ref, val in ((part_ref, part), (lse_ref, lse)):
                    for r in range(P4):
                        ref[0, pl.ds(r * chunk + q0, TQ_B // P4), :] = val[r * (TQ_B // P4):(r + 1) * (TQ_B // P4)]
            return (lambda: pieces(qb_ref, q0, TQ_B // P4), lambda: pieces(kb_ref, k0, WIN_B // P4),
                    lambda: pieces(vb_ref, k0, WIN_B // P4), kind, put)
        base = pl.multiple_of(g * chunk, chunk)
        if branch == 1:
            q0 = t * TQ_B
            k0 = min(max(q0 - BAND, 0), chunk - WIN_B)
            kind = 3 + (q0 - k0) // BAND

            def put(part, lse):
                part_ref[1, pl.ds(base + q0, TQ_B), :] = part
                lse_ref[1, pl.ds(base + q0, TQ_B), :] = lse
            return (lambda: qb_ref[pl.ds(base + q0, TQ_B), :], lambda: kb_ref[pl.ds(base + k0, WIN_B), :],
                    lambda: vb_ref[pl.ds(base + k0, WIN_B), :], kind, put)
        a, u = t // 2, t % 2
        q_rows = pl.ds(base + a + P4 * TQ_B * u, TQ_B, stride=P4)
        k_rows = pl.ds(base + a, WIN_B, stride=P4)

        def put(part, lse):
            parts = [part_ref[0, q_rows, :], part_ref[1, q_rows, :], part]
            lses = [lse_ref[0, q_rows, :], lse_ref[1, q_rows, :], lse]
            top = jnp.maximum(jnp.maximum(lses[0], lses[1]), lses[2])
            w = [jnp.exp2(l - top) for l in lses]
            num = w[0] * parts[0] + w[1] * parts[1] + w[2] * parts[2]
            merged_ref[q_rows, :] = num / (w[0] + w[1] + w[2])
        return (lambda: qf_ref[q_rows, :].astype(BF16), lambda: kf_ref[k_rows, :].astype(BF16),
                lambda: jnp.concatenate([vf_ref[k_rows, :].astype(BF16), jnp.ones((WIN_B, HEAD_DIM), BF16)], axis=1),
                6 + u, put)

    def start(branch, g, t):
        q, k, _, kind, _ = tile_rows(branch, g, t)
        s = lax.dot_general(q(), k(), (((1,), (1,)), ((), ())), preferred_element_type=F32) + bias_ref[kind]
        s_ref[t] = s
        m_ref[t] = jnp.broadcast_to(jnp.max(s, axis=-1, keepdims=True), (TQ_B, HEAD_DIM))

    def finish(branch, g, t):
        _, _, v, _, put = tile_rows(branch, g, t)
        m = m_ref[t]
        p = jnp.concatenate([jnp.exp2(s_ref[t, :, c:c + HEAD_DIM] - m) for c in range(0, WIN_B, HEAD_DIM)],
                            axis=1).astype(BF16)
        pv = jnp.dot(p, v(), preferred_element_type=F32)
        den = pv[:, HEAD_DIM:]
        put(pv[:, :HEAD_DIM] / den, m + jnp.log2(den))

    for t in range(g_tiles):
        start(0, 0, t)
    for branch in range(len(DILATIONS)):
        def body(g, carry, branch=branch):
            for t in range(g_tiles):
                finish(branch, g - 1, t)
                start(branch, g, t)
            return carry

        lax.fori_loop(1, n_groups, body, 0)
        for t in range(g_tiles):
            finish(branch, n_groups - 1, t)
            if branch + 1 < len(DILATIONS):
                start(branch + 1, 0, t)

    for r in range(P4):
        qf_ref[pl.ds(r, chunk, stride=P4), :] = merged_ref[r * chunk:(r + 1) * chunk, :]
    o_ref[...] = qf_ref[...].astype(o_ref.dtype)


def _mixer_b(qb, kb, vb, batch, seq):
    m = qb.shape[1]
    spec = pl.BlockSpec((None, seq, HEAD_DIM), lambda b, h: (h, b, 0))
    bias = _mixer_b_bias(seq)
    f32_rows = pltpu.VMEM((seq, HEAD_DIM), F32)
    bf16_rows = pltpu.VMEM((seq, HEAD_DIM), BF16)
    stored_branches = pltpu.VMEM((len(DILATIONS) - 1, seq, HEAD_DIM), F32)
    return pl.pallas_call(
        _mixer_b_kernel,
        grid=(batch, N_HEADS_B),
        in_specs=[spec, spec, spec, pl.BlockSpec(bias.shape, lambda b, h: (0, 0, 0))],
        out_specs=pl.BlockSpec((seq, HEAD_DIM), lambda b, h: (b, h)),
        out_shape=jax.ShapeDtypeStruct((m, WIDTH_B), BF16),
        scratch_shapes=[
            f32_rows, f32_rows, f32_rows,
            bf16_rows, bf16_rows, pltpu.VMEM((seq, 2 * HEAD_DIM), BF16),
            pltpu.VMEM((GROUP_TILES_B, TQ_B, WIN_B), F32),
            pltpu.VMEM((GROUP_TILES_B, TQ_B, HEAD_DIM), F32),
            stored_branches, stored_branches, f32_rows,
        ],
        compiler_params=_params(("parallel", "parallel"), 56),
        name="mixer_b",
    )(qb, kb, vb, bias)


def _outproj_kernel(oa_ref, ob_ref, x_ref, ga_ref, gb_ref, w_ref, gffn_ref, x1_ref, h2_ref):
    oa, ob = oa_ref[...].astype(F32), ob_ref[...].astype(F32)
    na = (oa * _rms_scale(oa) * ga_ref[...]).astype(BF16)
    nb = (ob * _rms_scale(ob) * gb_ref[...]).astype(BF16)
    x1 = (x_ref[...]
          + jnp.dot(na, w_ref[:WIDTH_A, :], preferred_element_type=F32)
          + jnp.dot(nb, w_ref[WIDTH_A:, :], preferred_element_type=F32))
    x1_ref[...] = x1
    h2_ref[...] = (x1 * _rms_scale(x1) * gffn_ref[...]).astype(BF16)


def _out_projection(oa, ob, x2, g_a, g_b, w_out, g_ffn):
    m = x2.shape[0]
    tm = TM_OUT
    row = lambda i: (i, 0)
    fixed = lambda i: (0, 0)
    return pl.pallas_call(
        _outproj_kernel,
        grid=(m // tm,),
        in_specs=[
            pl.BlockSpec((tm, WIDTH_A), row),
            pl.BlockSpec((tm, WIDTH_B), row),
            pl.BlockSpec((tm, D_MODEL), row),
            pl.BlockSpec((1, WIDTH_A), fixed),
            pl.BlockSpec((1, WIDTH_B), fixed),
            pl.BlockSpec((WIDTH_A + WIDTH_B, D_MODEL), fixed, pipeline_mode=pl.Buffered(1)),
            pl.BlockSpec((1, D_MODEL), fixed),
        ],
        out_specs=[pl.BlockSpec((tm, D_MODEL), row), pl.BlockSpec((tm, D_MODEL), row)],
        out_shape=[jax.ShapeDtypeStruct((m, D_MODEL), F32), jax.ShapeDtypeStruct((m, D_MODEL), BF16)],
        compiler_params=_params(("parallel",), 40),
        name="out_projection",
    )(oa, ob, x2, g_a, g_b, w_out, g_ffn)


def _ffn_kernel(nf, h_ref, x1_hbm, wg_ref, wu_ref, wd_ref, gfin_ref, o_ref, acc_ref, act_ref, x1_sem):
    j = pl.program_id(0)
    tm = acc_ref.shape[0]
    t_done = jnp.maximum(j - 1, 0)
    f_done = t_done % nf
    first, last = jnp.logical_and(j > 0, f_done == 0), jnp.logical_and(j > 0, f_done == nf - 1)

    def x1_copy():
        return pltpu.make_async_copy(x1_hbm.at[pl.ds((t_done // nf) * tm, tm), :], o_ref, x1_sem)

    @pl.when(j == 0)
    def _():
        act_ref[...] = jnp.zeros(act_ref.shape, BF16)
        acc_ref[...] = jnp.zeros(acc_ref.shape, F32)

    @pl.when(first)
    def _():
        x1_copy().start()

    for r in range(0, tm, FFN_SUB_ROWS):
        rows = slice(r, r + FFN_SUB_ROWS)
        part = jnp.dot(act_ref[rows, :], wd_ref[...], preferred_element_type=F32)
        acc_ref[rows, :] = jnp.where(f_done == 0, part, acc_ref[rows, :] + part)
        h = h_ref[rows, :]
        for c in range(0, wg_ref.shape[1], HEAD_DIM):
            w_pair = jnp.concatenate([wg_ref[:, c:c + HEAD_DIM], wu_ref[:, c:c + HEAD_DIM]], axis=1)
            gu = jnp.dot(h, w_pair, preferred_element_type=F32)
            gate, up = gu[:, :HEAD_DIM], gu[:, HEAD_DIM:]
            act_ref[rows, c:c + HEAD_DIM] = (gate * jax.nn.sigmoid(gate) * up).astype(BF16)

    @pl.when(last)
    def _():
        x1_copy().wait()
        g = gfin_ref[...]
        for r in range(0, tm, FFN_NORM_ROWS):
            rows = slice(r, r + FFN_NORM_ROWS)
            y = o_ref[rows, :] + acc_ref[rows, :]
            o_ref[rows, :] = y * _rms_scale(y) * g


def _ffn(h2, x1, w_gate_up, w_down, g_final):
    m = h2.shape[0]
    tm, tf = TM_FFN, TF_FFN
    nf = D_FF // tf
    n = (m // tm) * nf

    def started(j):
        t = jnp.minimum(j, n - 1)
        return t // nf, t % nf

    def finished(j):
        t = jnp.maximum(j - 1, 0)
        return t // nf, t % nf

    return pl.pallas_call(
        functools.partial(_ffn_kernel, nf),
        grid=(n + 1,),
        in_specs=[
            pl.BlockSpec((tm, D_MODEL), lambda j: (started(j)[0], 0)),
            pl.BlockSpec(memory_space=pl.ANY),
            pl.BlockSpec((D_MODEL, tf), lambda j: (0, started(j)[1])),
            pl.BlockSpec((D_MODEL, tf), lambda j: (0, started(j)[1] + nf)),
            pl.BlockSpec((tf, D_MODEL), lambda j: (finished(j)[1], 0)),
            pl.BlockSpec((1, D_MODEL), lambda j: (0, 0)),
        ],
        out_specs=pl.BlockSpec((tm, D_MODEL), lambda j: (finished(j)[0], 0)),
        out_shape=jax.ShapeDtypeStruct((m, D_MODEL), F32),
        scratch_shapes=[pltpu.VMEM((tm, D_MODEL), F32), pltpu.VMEM((tm, tf), BF16),
                        pltpu.SemaphoreType.DMA(())],
        compiler_params=_params(("arbitrary",), 56),
        name="ffn",
    )(h2, x1, w_gate_up, w_gate_up, w_down, g_final)


def kernel(x, g_mix, w_in, g_q_a, g_k_a, g_out_a, g_out_b, w_out, g_ffn, w_gate_up, w_down, g_final):
    batch, seq, d = x.shape
    assert d == D_MODEL and seq % GRID_W == 0
    assert seq % TM_PROJ == 0 and seq == P4 * GROUP_TILES_B * TQ_B and seq == max(DILATIONS) * WIN_B
    depth = w_in.shape[0]
    tables = _rope_tables(seq)
    x2 = x.reshape(batch * seq, d)
    for layer in range(depth):
        qa, ka, va, qb, kb, vb = _in_projection(
            x2, g_mix[layer][None], w_in[layer].astype(BF16),
            g_q_a[layer][None], g_k_a[layer][None], tables, seq)
        oa, (w_out_b, w_gate_up_b, w_down_b) = _mixer_a(
            qa, ka, va, batch, seq, (w_out[layer], w_gate_up[layer], w_down[layer]))
        ob = _mixer_b(qb, kb, vb, batch, seq)
        x1, h2 = _out_projection(oa, ob, x2, g_out_a[layer][None], g_out_b[layer][None],
                                 w_out_b, g_ffn[layer][None])
        assert depth == 1
        x2 = _ffn(h2, x1, w_gate_up_b, w_down_b, g_final[None])
    return x2.reshape(batch, seq, d)
```

```python
import functools
import math

import jax
import jax.numpy as jnp
import numpy as np
from jax import lax
from jax.experimental import pallas as pl
from jax.experimental.pallas import tpu as pltpu

F32 = jnp.float32
BF16 = jnp.bfloat16

D_MODEL = 2048
HEAD_DIM = 128
N_HEADS_A = 8
N_KV_A = 2
GROUP_A = N_HEADS_A // N_KV_A
N_HEADS_B = 8
WIDTH_A = N_HEADS_A * HEAD_DIM
WIDTH_B = N_HEADS_B * HEAD_DIM
KV_A = N_KV_A * HEAD_DIM
PROJ_OUT = WIDTH_A + 2 * KV_A + 3 * WIDTH_B
GRID_W = 64
AXIAL_THETA = 10000.0
ROPE_THETA = 500000.0
PARTIAL_ROT = HEAD_DIM // 4
DILATIONS = (1, 4, 16)
BAND = 64
D_FF = 5632
EPS = 1e-6
Q_SCALE = HEAD_DIM ** -0.5 * math.log2(math.e)

MIB = 1024 * 1024

TM_PROJ = 512
TQ_A = 512
TK_A = 512
BF16_TILE_ROWS = 16
TQ_B = 128
WIN_B = TQ_B + 2 * BAND
P4 = 4
GROUP_TILES_B = 8
TM_OUT = 512
TM_FFN = 1024
FFN_SUB_ROWS = 512
FFN_NORM_ROWS = 16
TF_FFN = 512


def _params(semantics, vmem_mib):
    return pltpu.CompilerParams(dimension_semantics=semantics, vmem_limit_bytes=vmem_mib * MIB)


def _rms_scale(y):
    return lax.rsqrt(jnp.mean(y * y, axis=-1, keepdims=True) + EPS)


def _rope_tables(seq):
    def angles(pos, dim, theta):
        inv = theta ** (-(np.arange(0, dim, 2, dtype=np.float64) / dim))
        return pos[:, None] * inv[None, :]

    half = HEAD_DIM // 2
    rows = seq // GRID_W
    row_pos = np.repeat(np.arange(rows, dtype=np.float64), GRID_W)
    col_pos = np.tile(np.arange(GRID_W, dtype=np.float64), rows)
    ar, ac = angles(row_pos, half, AXIAL_THETA), angles(col_pos, half, AXIAL_THETA)
    z = np.zeros_like(ar)
    cos_a = np.concatenate([np.cos(ar), np.cos(ar), np.cos(ac), np.cos(ac)], axis=-1)
    up_a = np.concatenate([-np.sin(ar), z, -np.sin(ac), z], axis=-1)
    dn_a = np.concatenate([z, np.sin(ar), z, np.sin(ac)], axis=-1)

    ab = angles(np.arange(seq, dtype=np.float64), PARTIAL_ROT, ROPE_THETA)
    hw = PARTIAL_ROT // 2
    one_rest = np.ones((seq, HEAD_DIM - PARTIAL_ROT))
    cos_b = np.concatenate([np.cos(ab), np.cos(ab), one_rest], axis=-1)
    up_b = np.concatenate([-np.sin(ab), np.zeros((seq, HEAD_DIM - hw))], axis=-1)
    dn_b = np.concatenate([np.zeros((seq, hw)), np.sin(ab), np.zeros((seq, HEAD_DIM - PARTIAL_ROT))], axis=-1)
    return tuple(jnp.asarray(t, dtype=F32) for t in (cos_a, up_a, dn_a, cos_b, up_b, dn_b))


def _rotate(t, cos, s_up, s_dn, shift):
    return (t * cos + pltpu.roll(t, HEAD_DIM - shift, 1) * s_up
            + pltpu.roll(t, shift, 1) * s_dn)


def _inproj_kernel(x_ref, gmix_ref, w_ref, gq_ref, gk_ref,
                   ca_ref, ua_ref, da_ref, cb_ref, ub_ref, db_ref,
                   qa_ref, ka_ref, va_ref, qb_ref, kb_ref, vb_ref):
    x = x_ref[...]
    h = (x * _rms_scale(x) * gmix_ref[...]).astype(BF16)
    ca, ua, da = ca_ref[...], ua_ref[...], da_ref[...]
    cb, ub, db = cb_ref[...], ub_ref[...], db_ref[...]
    gq, gk = gq_ref[...], gk_ref[...]
    ones = jnp.ones((x.shape[0], HEAD_DIM), BF16)

    pair = {}
    head_sum = (lax.broadcasted_iota(jnp.int32, (2 * HEAD_DIM, 2 * HEAD_DIM), 0) // HEAD_DIM
                == lax.broadcasted_iota(jnp.int32, (2 * HEAD_DIM, 2 * HEAD_DIM), 1) // HEAD_DIM
                ).astype(BF16)

    def head(col):
        base = col - col % (2 * HEAD_DIM)
        if base not in pair:
            pair.clear()
            pair[base] = jnp.dot(h, w_ref[:, base:base + 2 * HEAD_DIM], preferred_element_type=F32)
        return pair[base][:, col - base:col - base + HEAD_DIM]

    qa0, ka0, va0 = 0, WIDTH_A, WIDTH_A + KV_A
    qb0 = WIDTH_A + 2 * KV_A
    kb0, vb0 = qb0 + WIDTH_B, qb0 + 2 * WIDTH_B
    for i in range(N_HEADS_B):
        qb_ref[i] = _rotate(head(qb0 + i * HEAD_DIM), cb, ub, db, PARTIAL_ROT // 2) * Q_SCALE
    for i in range(N_HEADS_B):
        kb_ref[i] = _rotate(head(kb0 + i * HEAD_DIM), cb, ub, db, PARTIAL_ROT // 2)

    ya = [head(c) for c in range(qa0, va0, HEAD_DIM)]
    z = jnp.concatenate([jnp.concatenate([ya[i] * ya[i], ya[i + 1] * ya[i + 1]], axis=1)
                         for i in range(0, len(ya), 2)], axis=0)
    z_hi = z.astype(BF16)
    z_lo = (z - z_hi.astype(F32)).astype(BF16)
    ssq = (jnp.dot(z_hi, head_sum, preferred_element_type=F32)
           + jnp.dot(z_lo, head_sum, preferred_element_type=F32))
    r = lax.rsqrt(ssq * (1.0 / HEAD_DIM) + EPS)
    tm = x.shape[0]
    for i, y in enumerate(ya):
        g, scale = (gq, Q_SCALE) if i < N_HEADS_A else (gk, 1.0)
        ri = r[(i // 2) * tm:(i // 2 + 1) * tm, (i % 2) * HEAD_DIM:(i % 2 + 1) * HEAD_DIM]
        out = (_rotate(y * g, ca, ua, da, HEAD_DIM // 4) * (ri * scale)).astype(BF16)
        if i < N_HEADS_A:
            qa_ref[i] = out
        else:
            ka_ref[i - N_HEADS_A] = out
    for i in range(N_KV_A):
        va_ref[i, :, :HEAD_DIM] = head(va0 + i * HEAD_DIM).astype(BF16)
        va_ref[i, :, HEAD_DIM:] = ones
    for i in range(N_HEADS_B):
        vb_ref[i] = head(vb0 + i * HEAD_DIM)


def _in_projection(x2, g_mix, w_in, g_q, g_k, tables, seq):
    m = x2.shape[0]
    tm = TM_PROJ
    steps_per_seq = seq // tm
    row = lambda i: (i, 0)
    fixed = lambda i: (0, 0)
    tab = pl.BlockSpec((tm, HEAD_DIM), lambda i: (i % steps_per_seq, 0))
    heads = lambda n: pl.BlockSpec((n, tm, HEAD_DIM), lambda i: (0, i, 0))
    return pl.pallas_call(
        _inproj_kernel,
        grid=(m // tm,),
        in_specs=[
            pl.BlockSpec((tm, D_MODEL), row),
            pl.BlockSpec((1, D_MODEL), fixed),
            pl.BlockSpec((D_MODEL, PROJ_OUT), fixed, pipeline_mode=pl.Buffered(1)),
            pl.BlockSpec((1, HEAD_DIM), fixed),
            pl.BlockSpec((1, HEAD_DIM), fixed),
            tab, tab, tab, tab, tab, tab,
        ],
        out_specs=[
            heads(N_HEADS_A),
            heads(N_KV_A),
            pl.BlockSpec((N_KV_A, tm, 2 * HEAD_DIM), lambda i: (0, i, 0)),
            heads(N_HEADS_B), heads(N_HEADS_B), heads(N_HEADS_B),
        ],
        out_shape=[
            jax.ShapeDtypeStruct((N_HEADS_A, m, HEAD_DIM), BF16),
            jax.ShapeDtypeStruct((N_KV_A, m, HEAD_DIM), BF16),
            jax.ShapeDtypeStruct((N_KV_A, m, 2 * HEAD_DIM), BF16),
            jax.ShapeDtypeStruct((N_HEADS_B, m, HEAD_DIM), F32),
            jax.ShapeDtypeStruct((N_HEADS_B, m, HEAD_DIM), F32),
            jax.ShapeDtypeStruct((N_HEADS_B, m, HEAD_DIM), F32),
        ],
        compiler_params=_params(("parallel",), 56),
        name="in_projection",
    )(x2, g_mix, w_in, g_q, g_k, *tables)


def _mixer_a_kernel(cast_blocks, q_ref, k_ref, v_ref, *refs):
    nw = len(cast_blocks)
    w_src, o_ref, w_dst, (s_ref, m_ref) = refs[:nw], refs[nw], refs[nw + 1:2 * nw + 1], refs[2 * nw + 1:]
    tq = q_ref.shape[1]
    seq = k_ref.shape[1]
    step = pl.program_id(0)

    @pl.when(step == 0)
    def _():
        s_ref[...] = jnp.zeros(s_ref.shape, F32)
        m_ref[...] = jnp.zeros(m_ref.shape, F32)

    for src, dst, blocks in zip(w_src, w_dst, cast_blocks):
        @pl.when(step < blocks)
        def _(src=src, dst=dst):
            dst[...] = src[...].astype(BF16)

    q = jnp.concatenate([q_ref[h] for h in range(GROUP_A)], axis=0)
    m_done = m_ref[...]
    acc = None
    m_run = None
    for c in range(0, seq, TK_A):
        p = jnp.concatenate([jnp.exp2(s_ref[:, l:l + HEAD_DIM] - m_done) for l in range(c, c + TK_A, HEAD_DIM)],
                            axis=1).astype(BF16)
        pv = jnp.dot(p, v_ref[0, c:c + TK_A, :], preferred_element_type=F32)
        acc = pv if acc is None else acc + pv
        s = lax.dot_general(q, k_ref[0, c:c + TK_A, :], (((1,), (1,)), ((), ())),
                            preferred_element_type=F32)
        s_ref[:, c:c + TK_A] = s
        for l in range(0, TK_A, HEAD_DIM):
            part = s[:, l:l + HEAD_DIM]
            m_run = part if m_run is None else jnp.maximum(m_run, part)
    m_ref[...] = jnp.broadcast_to(jnp.max(m_run, axis=-1, keepdims=True), m_ref.shape)
    o = acc[:, :HEAD_DIM] / acc[:, HEAD_DIM:]
    for h in range(GROUP_A):
        o_ref[:, h * HEAD_DIM:(h + 1) * HEAD_DIM] = o[h * tq:(h + 1) * tq].astype(o_ref.dtype)


def _mixer_a(qa, ka, va, batch, seq, weights):
    m = qa.shape[1]
    tq = TQ_A
    tiles = seq // tq
    gw = GROUP_A * HEAD_DIM
    n = batch * N_KV_A * tiles
    def cast_rows(w):
        return next(r for r in range(BF16_TILE_ROWS, w.shape[0] + 1, BF16_TILE_ROWS)
                    if w.shape[0] % r == 0 and w.shape[0] // r <= n + 1)

    cast_blocks = tuple(w.shape[0] // cast_rows(w) for w in weights)
    w_specs = [pl.BlockSpec((cast_rows(w), w.shape[1]), lambda j, nb=nb: (jnp.minimum(j, nb - 1), 0))
               for w, nb in zip(weights, cast_blocks)]

    def coords(t):
        bg, i = t // tiles, t % tiles
        b, g = bg // N_KV_A, bg % N_KV_A
        return b * tiles + i, g, b

    def started(j):
        return coords(jnp.minimum(j, n - 1))

    def finished(j):
        return coords(jnp.maximum(j - 1, 0))

    oa, *w_bf16 = pl.pallas_call(
        functools.partial(_mixer_a_kernel, cast_blocks),
        grid=(n + 1,),
        in_specs=[
            pl.BlockSpec((GROUP_A, tq, HEAD_DIM), lambda j: (started(j)[1], started(j)[0], 0)),
            pl.BlockSpec((1, seq, HEAD_DIM), lambda j: (started(j)[1], started(j)[2], 0)),
            pl.BlockSpec((1, seq, 2 * HEAD_DIM), lambda j: (finished(j)[1], finished(j)[2], 0)),
        ] + w_specs,
        out_specs=[pl.BlockSpec((tq, gw), lambda j: finished(j)[:2])] + w_specs,
        out_shape=[jax.ShapeDtypeStruct((m, WIDTH_A), BF16)]
        + [jax.ShapeDtypeStruct(w.shape, BF16) for w in weights],
        scratch_shapes=[pltpu.VMEM((GROUP_A * tq, seq), F32),
                        pltpu.VMEM((GROUP_A * tq, HEAD_DIM), F32)],
        compiler_params=_params(("arbitrary",), 58),
        name="mixer_a",
    )(qa, ka, va, *weights)
    return oa, w_bf16


def _mixer_b_bias(seq):
    i = np.arange(TQ_B)[:, None]
    j = np.arange(WIN_B)[None, :]
    q4, k4 = TQ_B // P4, WIN_B // P4
    kinds = [P4 * (off + i % q4 - j % k4) + i // q4 - j // k4 for off in (0, BAND // P4, 2 * BAND // P4)]
    kinds += [off + i - j for off in (0, BAND, 2 * BAND)]
    kinds += [off + i - j for off in (0, TQ_B)]
    return jnp.asarray(np.stack([np.where(np.abs(dist) <= BAND, 0.0, -np.inf) for dist in kinds]), dtype=F32)


def _mixer_b_kernel(q_ref, k_ref, v_ref, bias_ref, o_ref,
                    qf_ref, kf_ref, vf_ref, qb_ref, kb_ref, vb_ref, s_ref, m_ref, part_ref, lse_ref, merged_ref):
    seq = q_ref.shape[0]
    chunk = seq // P4
    g_tiles = s_ref.shape[0]
    n_groups = seq // TQ_B // g_tiles
    assert chunk == g_tiles * TQ_B and chunk == P4 * WIN_B

    vb_ref[:, HEAD_DIM:] = jnp.ones((seq, HEAD_DIM), BF16)
    for r in range(P4):
        dst = slice(r * chunk, (r + 1) * chunk)
        for src, f_ref, b_ref in ((q_ref, qf_ref, qb_ref), (k_ref, kf_ref, kb_ref), (v_ref, vf_ref, vb_ref)):
            x = src[pl.ds(r, chunk, stride=P4), :]
            f_ref[dst, :] = x
            b_ref[dst, :HEAD_DIM] = x.astype(BF16)

    def pieces(ref, start, rows):
        return jnp.concatenate([ref[pl.ds(r * chunk + start, rows), :] for r in range(P4)], axis=0)

    def tile_rows(branch, g, t):
        if branch == 0:
            q0 = pl.multiple_of(g * (g_tiles * TQ_B // P4) + t * (TQ_B // P4), TQ_B // P4)
            k0 = pl.multiple_of(jnp.clip(q0 - BAND // P4, 0, chunk - WIN_B // P4), BAND // P4)
            kind = (q0 - k0) // (BAND // P4)

            def put(part, lse):
                for ref, val in ((part_ref, part), (lse_ref, lse)):
                    for r in range(P4):
                        ref[0, pl.ds(r * chunk + q0, TQ_B // P4), :] = val[r * (TQ_B // P4):(r + 1) * (TQ_B // P4)]
            return (lambda: pieces(qb_ref, q0, TQ_B // P4), lambda: pieces(kb_ref, k0, WIN_B // P4),
                    lambda: pieces(vb_ref, k0, WIN_B // P4), kind, put)
        base = pl.multiple_of(g * chunk, chunk)
        if branch == 1:
            q0 = t * TQ_B
            k0 = min(max(q0 - BAND, 0), chunk - WIN_B)
            kind = 3 + (q0 - k0) // BAND

            def put(part, lse):
                part_ref[1, pl.ds(base + q0, TQ_B), :] = part
                lse_ref[1, pl.ds(base + q0, TQ_B), :] = lse
            return (lambda: qb_ref[pl.ds(base + q0, TQ_B), :], lambda: kb_ref[pl.ds(base + k0, WIN_B), :],
                    lambda: vb_ref[pl.ds(base + k0, WIN_B), :], kind, put)
        a, u = t // 2, t % 2
        q_rows = pl.ds(base + a + P4 * TQ_B * u, TQ_B, stride=P4)
        k_rows = pl.ds(base + a, WIN_B, stride=P4)

        def put(part, lse):
            parts = [part_ref[0, q_rows, :], part_ref[1, q_rows, :], part]
            lses = [lse_ref[0, q_rows, :], lse_ref[1, q_rows, :], lse]
            top = jnp.maximum(jnp.maximum(lses[0], lses[1]), lses[2])
            w = [jnp.exp2(l - top) for l in lses]
            num = w[0] * parts[0] + w[1] * parts[1] + w[2] * parts[2]
            merged_ref[q_rows, :] = num / (w[0] + w[1] + w[2])
        return (lambda: qf_ref[q_rows, :].astype(BF16), lambda: kf_ref[k_rows, :].astype(BF16),
                lambda: jnp.concatenate([vf_ref[k_rows, :].astype(BF16), jnp.ones((WIN_B, HEAD_DIM), BF16)], axis=1),
                6 + u, put)

    def start(branch, g, t):
        q, k, _, kind, _ = tile_rows(branch, g, t)
        s = lax.dot_general(q(), k(), (((1,), (1,)), ((), ())), preferred_element_type=F32) + bias_ref[kind]
        s_ref[t] = s
        m_ref[t] = jnp.broadcast_to(jnp.max(s, axis=-1, keepdims=True), (TQ_B, HEAD_DIM))

    def finish(branch, g, t):
        _, _, v, _, put = tile_rows(branch, g, t)
        m = m_ref[t]
        p = jnp.concatenate([jnp.exp2(s_ref[t, :, c:c + HEAD_DIM] - m) for c in range(0, WIN_B, HEAD_DIM)],
                            axis=1).astype(BF16)
        pv = jnp.dot(p, v(), preferred_element_type=F32)
        den = pv[:, HEAD_DIM:]
        put(pv[:, :HEAD_DIM] / den, m + jnp.log2(den))

    for t in range(g_tiles):
        start(0, 0, t)
    for branch in range(len(DILATIONS)):
        def body(g, carry, branch=branch):
            for t in range(g_tiles):
                finish(branch, g - 1, t)
                start(branch, g, t)
            return carry

        lax.fori_loop(1, n_groups, body, 0)
        for t in range(g_tiles):
            finish(branch, n_groups - 1, t)
            if branch + 1 < len(DILATIONS):
                start(branch + 1, 0, t)

    for r in range(P4):
        qf_ref[pl.ds(r, chunk, stride=P4), :] = merged_ref[r * chunk:(r + 1) * chunk, :]
    o_ref[...] = qf_ref[...].astype(o_ref.dtype)


def _mixer_b(qb, kb, vb, batch, seq):
    m = qb.shape[1]
    spec = pl.BlockSpec((None, seq, HEAD_DIM), lambda b, h: (h, b, 0))
    bias = _mixer_b_bias(seq)
    f32_rows = pltpu.VMEM((seq, HEAD_DIM), F32)
    bf16_rows = pltpu.VMEM((seq, HEAD_DIM), BF16)
    stored_branches = pltpu.VMEM((len(DILATIONS) - 1, seq, HEAD_DIM), F32)
    return pl.pallas_call(
        _mixer_b_kernel,
        grid=(batch, N_HEADS_B),
        in_specs=[spec, spec, spec, pl.BlockSpec(bias.shape, lambda b, h: (0, 0, 0))],
        out_specs=pl.BlockSpec((seq, HEAD_DIM), lambda b, h: (b, h)),
        out_shape=jax.ShapeDtypeStruct((m, WIDTH_B), BF16),
        scratch_shapes=[
            f32_rows, f32_rows, f32_rows,
            bf16_rows, bf16_rows, pltpu.VMEM((seq, 2 * HEAD_DIM), BF16),
            pltpu.VMEM((GROUP_TILES_B, TQ_B, WIN_B), F32),
            pltpu.VMEM((GROUP_TILES_B, TQ_B, HEAD_DIM), F32),
            stored_branches, stored_branches, f32_rows,
        ],
        compiler_params=_params(("parallel", "parallel"), 56),
        name="mixer_b",
    )(qb, kb, vb, bias)


def _outproj_kernel(oa_ref, ob_ref, x_ref, ga_ref, gb_ref, w_ref, gffn_ref, x1_ref, h2_ref):
    oa, ob = oa_ref[...].astype(F32), ob_ref[...].astype(F32)
    na = (oa * _rms_scale(oa) * ga_ref[...]).astype(BF16)
    nb = (ob * _rms_scale(ob) * gb_ref[...]).astype(BF16)
    x1 = (x_ref[...]
          + jnp.dot(na, w_ref[:WIDTH_A, :], preferred_element_type=F32)
          + jnp.dot(nb, w_ref[WIDTH_A:, :], preferred_element_type=F32))
    x1_ref[...] = x1
    h2_ref[...] = (x1 * _rms_scale(x1) * gffn_ref[...]).astype(BF16)


def _out_projection(oa, ob, x2, g_a, g_b, w_out, g_ffn):
    m = x2.shape[0]
    tm = TM_OUT
    row = lambda i: (i, 0)
    fixed = lambda i: (0, 0)
    return pl.pallas_call(
        _outproj_kernel,
        grid=(m // tm,),
        in_specs=[
            pl.BlockSpec((tm, WIDTH_A), row),
            pl.BlockSpec((tm, WIDTH_B), row),
            pl.BlockSpec((tm, D_MODEL), row),
            pl.BlockSpec((1, WIDTH_A), fixed),
            pl.BlockSpec((1, WIDTH_B), fixed),
            pl.BlockSpec((WIDTH_A + WIDTH_B, D_MODEL), fixed, pipeline_mode=pl.Buffered(1)),
            pl.BlockSpec((1, D_MODEL), fixed),
        ],
        out_specs=[pl.BlockSpec((tm, D_MODEL), row), pl.BlockSpec((tm, D_MODEL), row)],
        out_shape=[jax.ShapeDtypeStruct((m, D_MODEL), F32), jax.ShapeDtypeStruct((m, D_MODEL), BF16)],
        compiler_params=_params(("parallel",), 40),
        name="out_projection",
    )(oa, ob, x2, g_a, g_b, w_out, g_ffn)


def _ffn_kernel(nf, h_ref, x1_hbm, wg_ref, wu_ref, wd_ref, gfin_ref, o_ref, acc_ref, act_ref, x1_sem):
    j = pl.program_id(0)
    tm = acc_ref.shape[0]
    t_done = jnp.maximum(j - 1, 0)
    f_done = t_done % nf
    first, last = jnp.logical_and(j > 0, f_done == 0), jnp.logical_and(j > 0, f_done == nf - 1)

    def x1_copy():
        return pltpu.make_async_copy(x1_hbm.at[pl.ds((t_done // nf) * tm, tm), :], o_ref, x1_sem)

    @pl.when(j == 0)
    def _():
        act_ref[...] = jnp.zeros(act_ref.shape, BF16)
        acc_ref[...] = jnp.zeros(acc_ref.shape, F32)

    @pl.when(first)
    def _():
        x1_copy().start()

    for r in range(0, tm, FFN_SUB_ROWS):
        rows = slice(r, r + FFN_SUB_ROWS)
        part = jnp.dot(act_ref[rows, :], wd_ref[...], preferred_element_type=F32)
        acc_ref[rows, :] = jnp.where(f_done == 0, part, acc_ref[rows, :] + part)
        h = h_ref[rows, :]
        for c in range(0, wg_ref.shape[1], HEAD_DIM):
            w_pair = jnp.concatenate([wg_ref[:, c:c + HEAD_DIM], wu_ref[:, c:c + HEAD_DIM]], axis=1)
            gu = jnp.dot(h, w_pair, preferred_element_type=F32)
            gate, up = gu[:, :HEAD_DIM], gu[:, HEAD_DIM:]
            act_ref[rows, c:c + HEAD_DIM] = (gate * jax.nn.sigmoid(gate) * up).astype(BF16)

    @pl.when(last)
    def _():
        x1_copy().wait()
        g = gfin_ref[...]
        for r in range(0, tm, FFN_NORM_ROWS):
            rows = slice(r, r + FFN_NORM_ROWS)
            y = o_ref[rows, :] + acc_ref[rows, :]
            o_ref[rows, :] = y * _rms_scale(y) * g


def _ffn(h2, x1, w_gate_up, w_down, g_final):
    m = h2.shape[0]
    tm, tf = TM_FFN, TF_FFN
    nf = D_FF // tf
    n = (m // tm) * nf

    def started(j):
        t = jnp.minimum(j, n - 1)
        return t // nf, t % nf

    def finished(j):
        t = jnp.maximum(j - 1, 0)
        return t // nf, t % nf

    return pl.pallas_call(
        functools.partial(_ffn_kernel, nf),
        grid=(n + 1,),
        in_specs=[
            pl.BlockSpec((tm, D_MODEL), lambda j: (started(j)[0], 0)),
            pl.BlockSpec(memory_space=pl.ANY),
            pl.BlockSpec((D_MODEL, tf), lambda j: (0, started(j)[1])),
            pl.BlockSpec((D_MODEL, tf), lambda j: (0, started(j)[1] + nf)),
            pl.BlockSpec((tf, D_MODEL), lambda j: (finished(j)[1], 0)),
            pl.BlockSpec((1, D_MODEL), lambda j: (0, 0)),
        ],
        out_specs=pl.BlockSpec((tm, D_MODEL), lambda j: (finished(j)[0], 0)),
        out_shape=jax.ShapeDtypeStruct((m, D_MODEL), F32),
        scratch_shapes=[pltpu.VMEM((tm, D_MODEL), F32), pltpu.VMEM((tm, tf), BF16),
                        pltpu.SemaphoreType.DMA(())],
        compiler_params=_params(("arbitrary",), 56),
        name="ffn",
    )(h2, x1, w_gate_up, w_gate_up, w_down, g_final)


def kernel(x, g_mix, w_in, g_q_a, g_k_a, g_out_a, g_out_b, w_out, g_ffn, w_gate_up, w_down, g_final):
    batch, seq, d = x.shape
    assert d == D_MODEL and seq % GRID_W == 0
    assert seq % TM_PROJ == 0 and seq == P4 * GROUP_TILES_B * TQ_B and seq == max(DILATIONS) * WIN_B
    depth = w_in.shape[0]
    tables = _rope_tables(seq)
    x2 = x.reshape(batch * seq, d)
    for layer in range(depth):
        qa, ka, va, qb, kb, vb = _in_projection(
            x2, g_mix[layer][None], w_in[layer].astype(BF16),
            g_q_a[layer][None], g_k_a[layer][None], tables, seq)
        oa, (w_out_b, w_gate_up_b, w_down_b) = _mixer_a(
            qa, ka, va, batch, seq, (w_out[layer], w_gate_up[layer], w_down[layer]))
        ob = _mixer_b(qb, kb, vb, batch, seq)
        x1, h2 = _out_projection(oa, ob, x2, g_out_a[layer][None], g_out_b[layer][None],
                                 w_out_b, g_ffn[layer][None])
        assert depth == 1
        x2 = _ffn(h2, x1, w_gate_up_b, w_down_b, g_final[None])
    return x2.reshape(batch, seq, d)
```

```python
import functools
import math

import jax
import jax.numpy as jnp
import numpy as np
from jax import lax
from jax.experimental import pallas as pl
from jax.experimental.pallas import tpu as pltpu

F32 = jnp.float32
BF16 = jnp.bfloat16

D_MODEL = 2048
HEAD_DIM = 128
N_HEADS_A = 8
N_KV_A = 2
GROUP_A = N_HEADS_A // N_KV_A
N_HEADS_B = 8
WIDTH_A = N_HEADS_A * HEAD_DIM
WIDTH_B = N_HEADS_B * HEAD_DIM
KV_A = N_KV_A * HEAD_DIM
PROJ_OUT = WIDTH_A + 2 * KV_A + 3 * WIDTH_B
GRID_W = 64
AXIAL_THETA = 10000.0
ROPE_THETA = 500000.0
PARTIAL_ROT = HEAD_DIM // 4
DILATIONS = (1, 4, 16)
BAND = 64
D_FF = 5632
EPS = 1e-6
Q_SCALE = HEAD_DIM ** -0.5 * math.log2(math.e)

MIB = 1024 * 1024

TM_PROJ = 512
TQ_A = 512
TK_A = 512
BF16_TILE_ROWS = 16
TQ_B = 128
WIN_B = TQ_B + 2 * BAND
P4 = 4
GROUP_TILES_B = 8
TM_OUT = 512
TM_FFN = 1024
FFN_SUB_ROWS = 512
FFN_NORM_ROWS = 16
TF_FFN = 512


def _params(semantics, vmem_mib):
    return pltpu.CompilerParams(dimension_semantics=semantics, vmem_limit_bytes=vmem_mib * MIB)


def _rms_scale(y):
    return lax.rsqrt(jnp.mean(y * y, axis=-1, keepdims=True) + EPS)


def _rope_tables(seq):
    def angles(pos, dim, theta):
        inv = theta ** (-(np.arange(0, dim, 2, dtype=np.float64) / dim))
        return pos[:, None] * inv[None, :]

    half = HEAD_DIM // 2
    rows = seq // GRID_W
    row_pos = np.repeat(np.arange(rows, dtype=np.float64), GRID_W)
    col_pos = np.tile(np.arange(GRID_W, dtype=np.float64), rows)
    ar, ac = angles(row_pos, half, AXIAL_THETA), angles(col_pos, half, AXIAL_THETA)
    z = np.zeros_like(ar)
    cos_a = np.concatenate([np.cos(ar), np.cos(ar), np.cos(ac), np.cos(ac)], axis=-1)
    up_a = np.concatenate([-np.sin(ar), z, -np.sin(ac), z], axis=-1)
    dn_a = np.concatenate([z, np.sin(ar), z, np.sin(ac)], axis=-1)

    ab = angles(np.arange(seq, dtype=np.float64), PARTIAL_ROT, ROPE_THETA)
    hw = PARTIAL_ROT // 2
    one_rest = np.ones((seq, HEAD_DIM - PARTIAL_ROT))
    cos_b = np.concatenate([np.cos(ab), np.cos(ab), one_rest], axis=-1)
    up_b = np.concatenate([-np.sin(ab), np.zeros((seq, HEAD_DIM - hw))], axis=-1)
    dn_b = np.concatenate([np.zeros((seq, hw)), np.sin(ab), np.zeros((seq, HEAD_DIM - PARTIAL_ROT))], axis=-1)
    return tuple(jnp.asarray(t, dtype=F32) for t in (cos_a, up_a, dn_a, cos_b, up_b, dn_b))


def _rotate(t, cos, s_up, s_dn, shift):
    return (t * cos + pltpu.roll(t, HEAD_DIM - shift, 1) * s_up
            + pltpu.roll(t, shift, 1) * s_dn)


def _inproj_kernel(x_ref, gmix_ref, w_ref, gq_ref, gk_ref,
                   ca_ref, ua_ref, da_ref, cb_ref, ub_ref, db_ref,
                   qa_ref, ka_ref, va_ref, qb_ref, kb_ref, vb_ref):
    x = x_ref[...]
    h = (x * _rms_scale(x) * gmix_ref[...]).astype(BF16)
    ca, ua, da = ca_ref[...], ua_ref[...], da_ref[...]
    cb, ub, db = cb_ref[...], ub_ref[...], db_ref[...]
    gq, gk = gq_ref[...], gk_ref[...]
    ones = jnp.ones((x.shape[0], HEAD_DIM), BF16)

    pair = {}
    head_sum = (lax.broadcasted_iota(jnp.int32, (2 * HEAD_DIM, 2 * HEAD_DIM), 0) // HEAD_DIM
                == lax.broadcasted_iota(jnp.int32, (2 * HEAD_DIM, 2 * HEAD_DIM), 1) // HEAD_DIM
                ).astype(BF16)

    def head(col):
        base = col - col % (2 * HEAD_DIM)
        if base not in pair:
            pair.clear()
            pair[base] = jnp.dot(h, w_ref[:, base:base + 2 * HEAD_DIM], preferred_element_type=F32)
        return pair[base][:, col - base:col - base + HEAD_DIM]

    qa0, ka0, va0 = 0, WIDTH_A, WIDTH_A + KV_A
    qb0 = WIDTH_A + 2 * KV_A
    kb0, vb0 = qb0 + WIDTH_B, qb0 + 2 * WIDTH_B
    for i in range(N_HEADS_B):
        qb_ref[i] = _rotate(head(qb0 + i * HEAD_DIM), cb, ub, db, PARTIAL_ROT // 2) * Q_SCALE
    for i in range(N_HEADS_B):
        kb_ref[i] = _rotate(head(kb0 + i * HEAD_DIM), cb, ub, db, PARTIAL_ROT // 2)

    ya = [head(c) for c in range(qa0, va0, HEAD_DIM)]
    z = jnp.concatenate([jnp.concatenate([ya[i] * ya[i], ya[i + 1] * ya[i + 1]], axis=1)
                         for i in range(0, len(ya), 2)], axis=0)
    z_hi = z.astype(BF16)
    z_lo = (z - z_hi.astype(F32)).astype(BF16)
    ssq = (jnp.dot(z_hi, head_sum, preferred_element_type=F32)
           + jnp.dot(z_lo, head_sum, preferred_element_type=F32))
    r = lax.rsqrt(ssq * (1.0 / HEAD_DIM) + EPS)
    tm = x.shape[0]
    for i, y in enumerate(ya):
        g, scale = (gq, Q_SCALE) if i < N_HEADS_A else (gk, 1.0)
        ri = r[(i // 2) * tm:(i // 2 + 1) * tm, (i % 2) * HEAD_DIM:(i % 2 + 1) * HEAD_DIM]
        out = (_rotate(y * g, ca, ua, da, HEAD_DIM // 4) * (ri * scale)).astype(BF16)
        if i < N_HEADS_A:
            qa_ref[i] = out
        else:
            ka_ref[i - N_HEADS_A] = out
    for i in range(N_KV_A):
        va_ref[i, :, :HEAD_DIM] = head(va0 + i * HEAD_DIM).astype(BF16)
        va_ref[i, :, HEAD_DIM:] = ones
    for i in range(N_HEADS_B):
        vb_ref[i] = head(vb0 + i * HEAD_DIM)


def _in_projection(x2, g_mix, w_in, g_q, g_k, tables, seq):
    m = x2.shape[0]
    tm = TM_PROJ
    steps_per_seq = seq // tm
    row = lambda i: (i, 0)
    fixed = lambda i: (0, 0)
    tab = pl.BlockSpec((tm, HEAD_DIM), lambda i: (i % steps_per_seq, 0))
    heads = lambda n: pl.BlockSpec((n, tm, HEAD_DIM), lambda i: (0, i, 0))
    return pl.pallas_call(
        _inproj_kernel,
        grid=(m // tm,),
        in_specs=[
            pl.BlockSpec((tm, D_MODEL), row),
            pl.BlockSpec((1, D_MODEL), fixed),
            pl.BlockSpec((D_MODEL, PROJ_OUT), fixed, pipeline_mode=pl.Buffered(1)),
            pl.BlockSpec((1, HEAD_DIM), fixed),
            pl.BlockSpec((1, HEAD_DIM), fixed),
            tab, tab, tab, tab, tab, tab,
        ],
        out_specs=[
            heads(N_HEADS_A),
            heads(N_KV_A),
            pl.BlockSpec((N_KV_A, tm, 2 * HEAD_DIM), lambda i: (0, i, 0)),
            heads(N_HEADS_B), heads(N_HEADS_B), heads(N_HEADS_B),
        ],
        out_shape=[
            jax.ShapeDtypeStruct((N_HEADS_A, m, HEAD_DIM), BF16),
            jax.ShapeDtypeStruct((N_KV_A, m, HEAD_DIM), BF16),
            jax.ShapeDtypeStruct((N_KV_A, m, 2 * HEAD_DIM), BF16),
            jax.ShapeDtypeStruct((N_HEADS_B, m, HEAD_DIM), F32),
            jax.ShapeDtypeStruct((N_HEADS_B, m, HEAD_DIM), F32),
            jax.ShapeDtypeStruct((N_HEADS_B, m, HEAD_DIM), F32),
        ],
        compiler_params=_params(("parallel",), 56),
        name="in_projection",
    )(x2, g_mix, w_in, g_q, g_k, *tables)


def _mixer_a_kernel(cast_blocks, q_ref, k_ref, v_ref, *refs):
    nw = len(cast_blocks)
    w_src, o_ref, w_dst, (s_ref, m_ref) = refs[:nw], refs[nw], refs[nw + 1:2 * nw + 1], refs[2 * nw + 1:]
    tq = q_ref.shape[1]
    seq = k_ref.shape[1]
    step = pl.program_id(0)

    @pl.when(step == 0)
    def _():
        s_ref[...] = jnp.zeros(s_ref.shape, F32)
        m_ref[...] = jnp.zeros(m_ref.shape, F32)

    for src, dst, blocks in zip(w_src, w_dst, cast_blocks):
        @pl.when(step < blocks)
        def _(src=src, dst=dst):
            dst[...] = src[...].astype(BF16)

    q = jnp.concatenate([q_ref[h] for h in range(GROUP_A)], axis=0)
    m_done = m_ref[...]
    acc = None
    m_run = None
    for c in range(0, seq, TK_A):
        p = jnp.concatenate([jnp.exp2(s_ref[:, l:l + HEAD_DIM] - m_done) for l in range(c, c + TK_A, HEAD_DIM)],
                            axis=1).astype(BF16)
        pv = jnp.dot(p, v_ref[0, c:c + TK_A, :], preferred_element_type=F32)
        acc = pv if acc is None else acc + pv
        s = lax.dot_general(q, k_ref[0, c:c + TK_A, :], (((1,), (1,)), ((), ())),
                            preferred_element_type=F32)
        s_ref[:, c:c + TK_A] = s
        for l in range(0, TK_A, HEAD_DIM):
            part = s[:, l:l + HEAD_DIM]
            m_run = part if m_run is None else jnp.maximum(m_run, part)
    m_ref[...] = jnp.broadcast_to(jnp.max(m_run, axis=-1, keepdims=True), m_ref.shape)
    o = acc[:, :HEAD_DIM] / acc[:, HEAD_DIM:]
    for h in range(GROUP_A):
        o_ref[:, h * HEAD_DIM:(h + 1) * HEAD_DIM] = o[h * tq:(h + 1) * tq].astype(o_ref.dtype)


def _mixer_a(qa, ka, va, batch, seq, weights):
    m = qa.shape[1]
    tq = TQ_A
    tiles = seq // tq
    gw = GROUP_A * HEAD_DIM
    n = batch * N_KV_A * tiles
    def cast_rows(w):
        return next(r for r in range(BF16_TILE_ROWS, w.shape[0] + 1, BF16_TILE_ROWS)
                    if w.shape[0] % r == 0 and w.shape[0] // r <= n + 1)

    cast_blocks = tuple(w.shape[0] // cast_rows(w) for w in weights)
    w_specs = [pl.BlockSpec((cast_rows(w), w.shape[1]), lambda j, nb=nb: (jnp.minimum(j, nb - 1), 0))
               for w, nb in zip(weights, cast_blocks)]

    def coords(t):
        bg, i = t // tiles, t % tiles
        b, g = bg // N_KV_A, bg % N_KV_A
        return b * tiles + i, g, b

    def started(j):
        return coords(jnp.minimum(j, n - 1))

    def finished(j):
        return coords(jnp.maximum(j - 1, 0))

    oa, *w_bf16 = pl.pallas_call(
        functools.partial(_mixer_a_kernel, cast_blocks),
        grid=(n + 1,),
        in_specs=[
            pl.BlockSpec((GROUP_A, tq, HEAD_DIM), lambda j: (started(j)[1], started(j)[0], 0)),
            pl.BlockSpec((1, seq, HEAD_DIM), lambda j: (started(j)[1], started(j)[2], 0)),
            pl.BlockSpec((1, seq, 2 * HEAD_DIM), lambda j: (finished(j)[1], finished(j)[2], 0)),
        ] + w_specs,
        out_specs=[pl.BlockSpec((tq, gw), lambda j: finished(j)[:2])] + w_specs,
        out_shape=[jax.ShapeDtypeStruct((m, WIDTH_A), BF16)]
        + [jax.ShapeDtypeStruct(w.shape, BF16) for w in weights],
        scratch_shapes=[pltpu.VMEM((GROUP_A * tq, seq), F32),
                        pltpu.VMEM((GROUP_A * tq, HEAD_DIM), F32)],
        compiler_params=_params(("arbitrary",), 58),
        name="mixer_a",
    )(qa, ka, va, *weights)
    return oa, w_bf16


def _mixer_b_bias(seq):
    i = np.arange(TQ_B)[:, None]
    j = np.arange(WIN_B)[None, :]
    q4, k4 = TQ_B // P4, WIN_B // P4
    kinds = [P4 * (off + i % q4 - j % k4) + i // q4 - j // k4 for off in (0, BAND // P4, 2 * BAND // P4)]
    kinds += [off + i - j for off in (0, BAND, 2 * BAND)]
    kinds += [off + i - j for off in (0, TQ_B)]
    return jnp.asarray(np.stack([np.where(np.abs(dist) <= BAND, 0.0, -np.inf) for dist in kinds]), dtype=F32)


def _mixer_b_kernel(q_ref, k_ref, v_ref, bias_ref, o_ref,
                    qf_ref, kf_ref, vf_ref, qb_ref, kb_ref, vb_ref, s_ref, m_ref, part_ref, lse_ref, merged_ref):
    seq = q_ref.shape[0]
    chunk = seq // P4
    g_tiles = s_ref.shape[0]
    n_groups = seq // TQ_B // g_tiles
    assert chunk == g_tiles * TQ_B and chunk == P4 * WIN_B

    vb_ref[:, HEAD_DIM:] = jnp.ones((seq, HEAD_DIM), BF16)
    for r in range(P4):
        dst = slice(r * chunk, (r + 1) * chunk)
        for src, f_ref, b_ref in ((q_ref, qf_ref, qb_ref), (k_ref, kf_ref, kb_ref), (v_ref, vf_ref, vb_ref)):
            x = src[pl.ds(r, chunk, stride=P4), :]
            f_ref[dst, :] = x
            b_ref[dst, :HEAD_DIM] = x.astype(BF16)

    def pieces(ref, start, rows):
        return jnp.concatenate([ref[pl.ds(r * chunk + start, rows), :] for r in range(P4)], axis=0)

    def tile_rows(branch, g, t):
        if branch == 0:
            q0 = pl.multiple_of(g * (g_tiles * TQ_B // P4) + t * (TQ_B // P4), TQ_B // P4)
            k0 = pl.multiple_of(jnp.clip(q0 - BAND // P4, 0, chunk - WIN_B // P4), BAND // P4)
            kind = (q0 - k0) // (BAND // P4)

            def put(part, lse):
                for ref, val in ((part_ref, part), (lse_ref, lse)):
                    for r in range(P4):
                        ref[0, pl.ds(r * chunk + q0, TQ_B // P4), :] = val[r * (TQ_B // P4):(r + 1) * (TQ_B // P4)]
            return (lambda: pieces(qb_ref, q0, TQ_B // P4), lambda: pieces(kb_ref, k0, WIN_B // P4),
                    lambda: pieces(vb_ref, k0, WIN_B // P4), kind, put)
        base = pl.multiple_of(g * chunk, chunk)
        if branch == 1:
            q0 = t * TQ_B
            k0 = min(max(q0 - BAND, 0), chunk - WIN_B)
            kind = 3 + (q0 - k0) // BAND

            def put(part, lse):
                part_ref[1, pl.ds(base + q0, TQ_B), :] = part
                lse_ref[1, pl.ds(base + q0, TQ_B), :] = lse
            return (lambda: qb_ref[pl.ds(base + q0, TQ_B), :], lambda: kb_ref[pl.ds(base + k0, WIN_B), :],
                    lambda: vb_ref[pl.ds(base + k0, WIN_B), :], kind, put)
        a, u = t // 2, t % 2
        q_rows = pl.ds(base + a + P4 * TQ_B * u, TQ_B, stride=P4)
        k_rows = pl.ds(base + a, WIN_B, stride=P4)

        def put(part, lse):
            parts = [part_ref[0, q_rows, :], part_ref[1, q_rows, :], part]
            lses = [lse_ref[0, q_rows, :], lse_ref[1, q_rows, :], lse]
            top = jnp.maximum(jnp.maximum(lses[0], lses[1]), lses[2])
            w = [jnp.exp2(l - top) for l in lses]
            num = w[0] * parts[0] + w[1] * parts[1] + w[2] * parts[2]
            merged_ref[q_rows, :] = num / (w[0] + w[1] + w[2])
        return (lambda: qf_ref[q_rows, :].astype(BF16), lambda: kf_ref[k_rows, :].astype(BF16),
                lambda: jnp.concatenate([vf_ref[k_rows, :].astype(BF16), jnp.ones((WIN_B, HEAD_DIM), BF16)], axis=1),
                6 + u, put)

    def start(branch, g, t):
        q, k, _, kind, _ = tile_rows(branch, g, t)
        s = lax.dot_general(q(), k(), (((1,), (1,)), ((), ())), preferred_element_type=F32) + bias_ref[kind]
        s_ref[t] = s
        m_ref[t] = jnp.broadcast_to(jnp.max(s, axis=-1, keepdims=True), (TQ_B, HEAD_DIM))

    def finish(branch, g, t):
        _, _, v, _, put = tile_rows(branch, g, t)
        m = m_ref[t]
        p = jnp.concatenate([jnp.exp2(s_ref[t, :, c:c + HEAD_DIM] - m) for c in range(0, WIN_B, HEAD_DIM)],
                            axis=1).astype(BF16)
        pv = jnp.dot(p, v(), preferred_element_type=F32)
        den = pv[:, HEAD_DIM:]
        put(pv[:, :HEAD_DIM] / den, m + jnp.log2(den))

    for t in range(g_tiles):
        start(0, 0, t)
    for branch in range(len(DILATIONS)):
        def body(g, carry, branch=branch):
            for t in range(g_tiles):
                finish(branch, g - 1, t)
                start(branch, g, t)
            return carry

        lax.fori_loop(1, n_groups, body, 0)
        for t in range(g_tiles):
            finish(branch, n_groups - 1, t)
            if branch + 1 < len(DILATIONS):
                start(branch + 1, 0, t)

    for r in range(P4):
        qf_ref[pl.ds(r, chunk, stride=P4), :] = merged_ref[r * chunk:(r + 1) * chunk, :]
    o_ref[...] = qf_ref[...].astype(o_ref.dtype)


def _mixer_b(qb, kb, vb, batch, seq):
    m = qb.shape[1]
    spec = pl.BlockSpec((None, seq, HEAD_DIM), lambda b, h: (h, b, 0))
    bias = _mixer_b_bias(seq)
    f32_rows = pltpu.VMEM((seq, HEAD_DIM), F32)
    bf16_rows = pltpu.VMEM((seq, HEAD_DIM), BF16)
    stored_branches = pltpu.VMEM((len(DILATIONS) - 1, seq, HEAD_DIM), F32)
    return pl.pallas_call(
        _mixer_b_kernel,
        grid=(batch, N_HEADS_B),
        in_specs=[spec, spec, spec, pl.BlockSpec(bias.shape, lambda b, h: (0, 0, 0))],
        out_specs=pl.BlockSpec((seq, HEAD_DIM), lambda b, h: (b, h)),
        out_shape=jax.ShapeDtypeStruct((m, WIDTH_B), BF16),
        scratch_shapes=[
            f32_rows, f32_rows, f32_rows,
            bf16_rows, bf16_rows, pltpu.VMEM((seq, 2 * HEAD_DIM), BF16),
            pltpu.VMEM((GROUP_TILES_B, TQ_B, WIN_B), F32),
            pltpu.VMEM((GROUP_TILES_B, TQ_B, HEAD_DIM), F32),
            stored_branches, stored_branches, f32_rows,
        ],
        compiler_params=_params(("parallel", "parallel"), 56),
        name="mixer_b",
    )(qb, kb, vb, bias)


def _outproj_kernel(oa_ref, ob_ref, x_ref, ga_ref, gb_ref, w_ref, gffn_ref, x1_ref, h2_ref):
    oa, ob = oa_ref[...].astype(F32), ob_ref[...].astype(F32)
    na = (oa * _rms_scale(oa) * ga_ref[...]).astype(BF16)
    nb = (ob * _rms_scale(ob) * gb_ref[...]).astype(BF16)
    mixed = jnp.concatenate([na, nb], axis=1)
    x1 = x_ref[...] + jnp.dot(mixed, w_ref[...], preferred_element_type=F32)
    x1_ref[...] = x1
    h2_ref[...] = (x1 * _rms_scale(x1) * gffn_ref[...]).astype(BF16)


def _out_projection(oa, ob, x2, g_a, g_b, w_out, g_ffn):
    m = x2.shape[0]
    tm = TM_OUT
    row = lambda i: (i, 0)
    fixed = lambda i: (0, 0)
    return pl.pallas_call(
        _outproj_kernel,
        grid=(m // tm,),
        in_specs=[
            pl.BlockSpec((tm, WIDTH_A), row),
            pl.BlockSpec((tm, WIDTH_B), row),
            pl.BlockSpec((tm, D_MODEL), row),
            pl.BlockSpec((1, WIDTH_A), fixed),
            pl.BlockSpec((1, WIDTH_B), fixed),
            pl.BlockSpec((WIDTH_A + WIDTH_B, D_MODEL), fixed, pipeline_mode=pl.Buffered(1)),
            pl.BlockSpec((1, D_MODEL), fixed),
        ],
        out_specs=[pl.BlockSpec((tm, D_MODEL), row), pl.BlockSpec((tm, D_MODEL), row)],
        out_shape=[jax.ShapeDtypeStruct((m, D_MODEL), F32), jax.ShapeDtypeStruct((m, D_MODEL), BF16)],
        compiler_params=_params(("parallel",), 40),
        name="out_projection",
    )(oa, ob, x2, g_a, g_b, w_out, g_ffn)


def _ffn_kernel(nf, h_ref, x1_hbm, wg_ref, wu_ref, wd_ref, gfin_ref, o_ref, acc_ref, act_ref, x1_sem):
    j = pl.program_id(0)
    tm = acc_ref.shape[0]
    t_done = jnp.maximum(j - 1, 0)
    f_done = t_done % nf
    first, last = jnp.logical_and(j > 0, f_done == 0), jnp.logical_and(j > 0, f_done == nf - 1)

    def x1_copy():
        return pltpu.make_async_copy(x1_hbm.at[pl.ds((t_done // nf) * tm, tm), :], o_ref, x1_sem)

    @pl.when(j == 0)
    def _():
        act_ref[...] = jnp.zeros(act_ref.shape, BF16)
        acc_ref[...] = jnp.zeros(acc_ref.shape, F32)

    @pl.when(first)
    def _():
        x1_copy().start()

    for r in range(0, tm, FFN_SUB_ROWS):
        rows = slice(r, r + FFN_SUB_ROWS)
        part = jnp.dot(act_ref[rows, :], wd_ref[...], preferred_element_type=F32)
        acc_ref[rows, :] = jnp.where(f_done == 0, part, acc_ref[rows, :] + part)
        h = h_ref[rows, :]
        for c in range(0, wg_ref.shape[1], HEAD_DIM):
            w_pair = jnp.concatenate([wg_ref[:, c:c + HEAD_DIM], wu_ref[:, c:c + HEAD_DIM]], axis=1)
            gu = jnp.dot(h, w_pair, preferred_element_type=F32)
            gate, up = gu[:, :HEAD_DIM], gu[:, HEAD_DIM:]
            act_ref[rows, c:c + HEAD_DIM] = (gate * jax.nn.sigmoid(gate) * up).astype(BF16)

    @pl.when(last)
    def _():
        x1_copy().wait()
        g = gfin_ref[...]
        for r in range(0, tm, FFN_NORM_ROWS):
            rows = slice(r, r + FFN_NORM_ROWS)
            y = o_ref[rows, :] + acc_ref[rows, :]
            o_ref[rows, :] = y * _rms_scale(y) * g


def _ffn(h2, x1, w_gate_up, w_down, g_final):
    m = h2.shape[0]
    tm, tf = TM_FFN, TF_FFN
    nf = D_FF // tf
    n = (m // tm) * nf

    def started(j):
        t = jnp.minimum(j, n - 1)
        return t // nf, t % nf

    def finished(j):
        t = jnp.maximum(j - 1, 0)
        return t // nf, t % nf

    return pl.pallas_call(
        functools.partial(_ffn_kernel, nf),
        grid=(n + 1,),
        in_specs=[
            pl.BlockSpec((tm, D_MODEL), lambda j: (started(j)[0], 0)),
            pl.BlockSpec(memory_space=pl.ANY),
            pl.BlockSpec((D_MODEL, tf), lambda j: (0, started(j)[1])),
            pl.BlockSpec((D_MODEL, tf), lambda j: (0, started(j)[1] + nf)),
            pl.BlockSpec((tf, D_MODEL), lambda j: (finished(j)[1], 0)),
            pl.BlockSpec((1, D_MODEL), lambda j: (0, 0)),
        ],
        out_specs=pl.BlockSpec((tm, D_MODEL), lambda j: (finished(j)[0], 0)),
        out_shape=jax.ShapeDtypeStruct((m, D_MODEL), F32),
        scratch_shapes=[pltpu.VMEM((tm, D_MODEL), F32), pltpu.VMEM((tm, tf), BF16),
                        pltpu.SemaphoreType.DMA(())],
        compiler_params=_params(("arbitrary",), 56),
        name="ffn",
    )(h2, x1, w_gate_up, w_gate_up, w_down, g_final)


def kernel(x, g_mix, w_in, g_q_a, g_k_a, g_out_a, g_out_b, w_out, g_ffn, w_gate_up, w_down, g_final):
    batch, seq, d = x.shape
    assert d == D_MODEL and seq % GRID_W == 0
    assert seq % TM_PROJ == 0 and seq == P4 * GROUP_TILES_B * TQ_B and seq == max(DILATIONS) * WIN_B
    depth = w_in.shape[0]
    tables = _rope_tables(seq)
    x2 = x.reshape(batch * seq, d)
    for layer in range(depth):
        qa, ka, va, qb, kb, vb = _in_projection(
            x2, g_mix[layer][None], w_in[layer].astype(BF16),
            g_q_a[layer][None], g_k_a[layer][None], tables, seq)
        oa, (w_out_b, w_gate_up_b, w_down_b) = _mixer_a(
            qa, ka, va, batch, seq, (w_out[layer], w_gate_up[layer], w_down[layer]))
        ob = _mixer_b(qb, kb, vb, batch, seq)
        x1, h2 = _out_projection(oa, ob, x2, g_out_a[layer][None], g_out_b[layer][None],
                                 w_out_b, g_ffn[layer][None])
        assert depth == 1
        x2 = _ffn(h2, x1, w_gate_up_b, w_down_b, g_final[None])
    return x2.reshape(batch, seq, d)
```
